```python
import jax, jax.numpy as jnp
from jax import lax
import numpy as np

D_MODEL = 2048
BATCH = 4
SEQ = 4096
DEPTH = 4

N_MEM = 256
MIX_WIDTH = D_MODEL
FOX_WIDTH = MIX_WIDTH // 2
RWKV_WIDTH = MIX_WIDTH - FOX_WIDTH
HEAD_DIM = 64
FOX_HEADS = FOX_WIDTH // HEAD_DIM
RWKV_HEADS = RWKV_WIDTH // HEAD_DIM
DECAY_LORA = max(32, int(round(1.8 * RWKV_WIDTH ** 0.5 / 32)) * 32)
AAA_LORA = max(32, int(round(1.8 * RWKV_WIDTH ** 0.5 / 32)) * 32)
GATE_LORA = max(32, int(round(0.6 * RWKV_WIDTH ** 0.8 / 32)) * 32)
SHIFT_WIDTH = 3 * RWKV_WIDTH + DECAY_LORA + AAA_LORA + GATE_LORA
IN_WIDTH = 3 * FOX_WIDTH + FOX_HEADS + SHIFT_WIDTH
D_FF = ((8 * D_MODEL // 3 + 255) // 256) * 256
XA_HEADS = 4
XA_HEAD_DIM = D_MODEL // XA_HEADS
Q_BLOCK = 128
ALPHA = (2.0 * DEPTH) ** 0.25
BETA = (8.0 * DEPTH) ** -0.25
LN_EPS = 1e-5
GN_EPS = 64e-5
NEG_INF = -1e30

kernel_name = "fox_rwkv7_hymba_macaron_deepnorm"


def layer_norm(x, g, b):
    xf = x.astype(jnp.float32)
    mu = jnp.mean(xf, axis=-1, keepdims=True)
    var = jnp.mean(jnp.square(xf - mu), axis=-1, keepdims=True)
    return ((xf - mu) * lax.rsqrt(var + LN_EPS) * g + b).astype(x.dtype)


def swiglu(x, w13, w2):
    gate, up = jnp.split(x @ w13, 2, axis=-1)
    return (jax.nn.silu(gate) * up) @ w2


def fox_attention(q, k, v, logf):
    S_ = q.shape[1]
    c = jnp.swapaxes(jnp.cumsum(logf, axis=1), 1, 2)
    scale = HEAD_DIM ** -0.5
    outs = []
    for i in range(S_ // Q_BLOCK):
        lo, hi = i * Q_BLOCK, (i + 1) * Q_BLOCK
        s = jnp.einsum('bqhd,bkhd->bhqk', q[:, lo:hi], k[:, :hi],
                       preferred_element_type=jnp.float32) * scale
        s = s + c[:, :, lo:hi, None] - c[:, :, None, :hi]
        causal = jnp.arange(hi)[None, :] <= jnp.arange(lo, hi)[:, None]
        p = jax.nn.softmax(jnp.where(causal, s, NEG_INF), axis=-1)
        outs.append(jnp.einsum('bhqk,bkhd->bqhd', p.astype(v.dtype), v[:, :hi]))
    return jnp.concatenate(outs, axis=1)


def _rwkv7_step(state, inp):
    r, w, k, v, kk, a = inp
    sa = jnp.einsum('bhvk,bhk->bhv', state, -kk)
    state = state * w[:, :, None, :] + sa[..., None] * (kk * a)[:, :, None, :] + v[..., None] * k[:, :, None, :]
    y = jnp.einsum('bhvk,bhk->bhv', state, r)
    return state, y


def rwkv7_time_mix(u, mu, w0, w_up, a0, a_up, g_up, k_k, k_a, r_k, gn_g, gn_b):
    B_, S_, _ = u.shape
    u_prev = jnp.pad(u, ((0, 0), (1, 0), (0, 0)))[:, :-1]
    u = u + mu * (u_prev - u)
    cuts = [RWKV_WIDTH, 2 * RWKV_WIDTH, 3 * RWKV_WIDTH, 3 * RWKV_WIDTH + DECAY_LORA,
            3 * RWKV_WIDTH + DECAY_LORA + AAA_LORA]
    r, k, v, wd, ad, gd = jnp.split(u, cuts, axis=-1)
    w = -jax.nn.softplus(-(w0 + jnp.tanh(wd) @ w_up).astype(jnp.float32)) - 0.5
    decay = jnp.exp(-jnp.exp(w))
    a = jax.nn.sigmoid(a0 + ad @ a_up)
    g = jax.nn.sigmoid(gd) @ g_up
    hs = lambda t: t.reshape(B_, S_, RWKV_HEADS, HEAD_DIM).astype(jnp.float32)
    kk = hs(k * k_k)
    kk = kk / jnp.maximum(jnp.sqrt(jnp.sum(kk * kk, axis=-1, keepdims=True)), 1e-12)
    k = k * (1.0 + (a - 1.0) * k_a)
    r_h, k_h, v_h, a_h, w_h = hs(r), hs(k), hs(v), hs(a), hs(decay)
    tm = lambda t: jnp.swapaxes(t, 0, 1)
    state0 = jnp.zeros((B_, RWKV_HEADS, HEAD_DIM, HEAD_DIM), jnp.float32)
    _, y = lax.scan(_rwkv7_step, state0, (tm(r_h), tm(w_h), tm(k_h), tm(v_h), tm(kk), tm(a_h)))
    y = tm(y)
    ym = jnp.mean(y, axis=-1, keepdims=True)
    yv = jnp.mean(jnp.square(y - ym), axis=-1, keepdims=True)
    y = ((y - ym) * lax.rsqrt(yv + GN_EPS)).reshape(B_, S_, RWKV_WIDTH) * gn_g + gn_b
    bonus = (jnp.sum(r_h * k_h * r_k, axis=-1, keepdims=True) * v_h).reshape(B_, S_, RWKV_WIDTH)
    return ((y + bonus) * g).astype(u.dtype)


def memory_cross_attention(x, mem, wq, wkv, wo):
    B_, S_, _ = x.shape
    q = (x @ wq).reshape(B_, S_, XA_HEADS, XA_HEAD_DIM)
    mk, mv = jnp.split(mem @ wkv, 2, axis=-1)
    mk = mk.reshape(B_, -1, XA_HEADS, XA_HEAD_DIM)
    mv = mv.reshape(B_, -1, XA_HEADS, XA_HEAD_DIM)
    s = jnp.einsum('bqhd,bmhd->bhqm', q, mk, preferred_element_type=jnp.float32) * XA_HEAD_DIM ** -0.5
    p = jax.nn.softmax(s, axis=-1).astype(mv.dtype)
    o = jnp.einsum('bhqm,bmhd->bqhd', p, mv).reshape(B_, S_, D_MODEL)
    return o @ wo


def setup_inputs(seed: int = 0) -> dict:
    key = jax.random.key(seed)
    ks = jax.random.split(key, 32)
    f32 = jnp.float32
    nrm = lambda k, shape, s: jax.random.normal(k, shape, f32) * s
    L = DEPTH
    return {
        "x": nrm(ks[0], (BATCH, SEQ, D_MODEL), 1.0),
        "mem": nrm(ks[1], (BATCH, N_MEM, D_MODEL), 1.0),
        "ln_g": 1.0 + nrm(ks[2], (L, 4, D_MODEL), 0.02),
        "ln_b": nrm(ks[3], (L, 4, D_MODEL), 0.02),
        "ffn1_w13": nrm(ks[4], (L, D_MODEL, 2 * D_FF), D_MODEL ** -0.5),
        "ffn1_w2": nrm(ks[5], (L, D_FF, D_MODEL), BETA * D_FF ** -0.5),
        "w_in": nrm(ks[6], (L, D_MODEL, IN_WIDTH), D_MODEL ** -0.5),
        "fox_f_bias": 2.0 + nrm(ks[7], (L, FOX_HEADS), 0.1),
        "fox_out_g": 1.0 + nrm(ks[8], (L, FOX_WIDTH), 0.02),
        "rwkv_mu": jax.random.uniform(ks[9], (L, SHIFT_WIDTH), f32),
        "rwkv_w0": jnp.linspace(-6.0, -1.0, RWKV_WIDTH, dtype=f32)[None, :] + nrm(ks[10], (L, RWKV_WIDTH), 0.1),
        "rwkv_w_up": nrm(ks[11], (L, DECAY_LORA, RWKV_WIDTH), 0.1 * DECAY_LORA ** -0.5),
        "rwkv_a0": nrm(ks[12], (L, RWKV_WIDTH), 0.1),
        "rwkv_a_up": nrm(ks[13], (L, AAA_LORA, RWKV_WIDTH), 0.1 * AAA_LORA ** -0.5),
        "rwkv_g_up": nrm(ks[14], (L, GATE_LORA, RWKV_WIDTH), GATE_LORA ** -0.5),
        "rwkv_k_k": 0.85 + nrm(ks[15], (L, RWKV_WIDTH), 0.02),
        "rwkv_k_a": 1.0 + nrm(ks[16], (L, RWKV_WIDTH), 0.02),
        "rwkv_r_k": nrm(ks[17], (L, RWKV_HEADS, HEAD_DIM), 0.1),
        "rwkv_gn_g": 1.0 + nrm(ks[18], (L, RWKV_WIDTH), 0.02),
        "rwkv_gn_b": nrm(ks[19], (L, RWKV_WIDTH), 0.02),
        "w_mix_out": nrm(ks[20], (L, MIX_WIDTH, D_MODEL), BETA * MIX_WIDTH ** -0.5),
        "xa_wq": nrm(ks[21], (L, D_MODEL, D_MODEL), D_MODEL ** -0.5),
        "xa_wkv": nrm(ks[22], (L, D_MODEL, 2 * D_MODEL), D_MODEL ** -0.5),
        "xa_wo": nrm(ks[23], (L, D_MODEL, D_MODEL), BETA * D_MODEL ** -0.5),
        "ffn2_w13": nrm(ks[24], (L, D_MODEL, 2 * D_FF), D_MODEL ** -0.5),
        "ffn2_w2": nrm(ks[25], (L, D_FF, D_MODEL), BETA * D_FF ** -0.5),
    }


def reference(x, mem, ln_g, ln_b, ffn1_w13, ffn1_w2, w_in, fox_f_bias, fox_out_g, rwkv_mu, rwkv_w0,
              rwkv_w_up, rwkv_a0, rwkv_a_up, rwkv_g_up, rwkv_k_k, rwkv_k_a, rwkv_r_k, rwkv_gn_g, rwkv_gn_b,
              w_mix_out, xa_wq, xa_wkv, xa_wo, ffn2_w13, ffn2_w2):
    B_, S_, _ = x.shape
    c1 = 3 * FOX_WIDTH
    c2 = c1 + FOX_HEADS
    for l in range(DEPTH):
        x = layer_norm(ALPHA * x + 0.5 * swiglu(x, ffn1_w13[l], ffn1_w2[l]), ln_g[l, 0], ln_b[l, 0])
        proj = x @ w_in[l]
        fq, fk, fv = jnp.split(proj[..., :c1], 3, axis=-1)
        logf = jax.nn.log_sigmoid((proj[..., c1:c2] + fox_f_bias[l]).astype(jnp.float32))
        fh = lambda t: t.reshape(B_, S_, FOX_HEADS, HEAD_DIM)
        fo = fox_attention(fh(fq), fh(fk), fh(fv), logf).reshape(B_, S_, FOX_WIDTH).astype(jnp.float32)
        fo = (fo * lax.rsqrt(jnp.mean(fo * fo, axis=-1, keepdims=True) + 1e-6) * fox_out_g[l]).astype(x.dtype)
        ro = rwkv7_time_mix(proj[..., c2:], rwkv_mu[l], rwkv_w0[l], rwkv_w_up[l], rwkv_a0[l], rwkv_a_up[l],
                            rwkv_g_up[l], rwkv_k_k[l], rwkv_k_a[l], rwkv_r_k[l], rwkv_gn_g[l], rwkv_gn_b[l])
        mix = jnp.concatenate([fo, ro], axis=-1) @ w_mix_out[l]
        x = layer_norm(ALPHA * x + mix, ln_g[l, 1], ln_b[l, 1])
        xa = memory_cross_attention(x, mem, xa_wq[l], xa_wkv[l], xa_wo[l])
        x = layer_norm(ALPHA * x + xa, ln_g[l, 2], ln_b[l, 2])
        x = layer_norm(ALPHA * x + 0.5 * swiglu(x, ffn2_w13[l], ffn2_w2[l]), ln_g[l, 3], ln_b[l, 3])
    return x
```

```python
import functools

import jax
import jax.numpy as jnp
from jax import lax
from jax.experimental import pallas as pl
from jax.experimental.pallas import tpu as pltpu

HEAD_DIM = 64
XA_HEADS = 4
LN_EPS = 1e-5
GN_EPS = 64e-5
FOX_EPS = 1e-6
NEG_INF = -1e30
LANES = 128
CHUNK = 64
VMEM_LIMIT = 56 * 1024 * 1024

F32 = jnp.float32
BF16 = jnp.bfloat16


def _params(semantics):
    return pltpu.CompilerParams(dimension_semantics=semantics, vmem_limit_bytes=VMEM_LIMIT)


def _layer_norm(z, g, b):
    mu = jnp.mean(z, axis=-1, keepdims=True)
    zc = z - mu
    var = jnp.mean(zc * zc, axis=-1, keepdims=True)
    return zc * lax.rsqrt(var + LN_EPS) * g + b


def _split_dot(a, b01, terms):
    acc = None
    rem = a
    for _ in range(terms):
        part = rem.astype(BF16)
        d = jnp.dot(part, b01, preferred_element_type=F32)
        acc = d if acc is None else acc + d
        rem = rem - part.astype(F32)
    return acc


def _split_dot_rhs(a01, b, terms):
    acc = None
    rem = b
    for _ in range(terms):
        part = rem.astype(BF16)
        d = jnp.dot(a01, part, preferred_element_type=F32)
        acc = d if acc is None else acc + d
        rem = rem - part.astype(F32)
    return acc


def _dot_nt(a, b):
    return lax.dot_general(a, b, (((1,), (1,)), ((), ())), preferred_element_type=F32)


def _dot_tn(a, b):
    return lax.dot_general(a, b, (((0,), (0,)), ((), ())), preferred_element_type=F32)


def _ffn_kernel(x_ref, wg_ref, wu_ref, w2_ref, g_ref, b_ref, o_ref, ob_ref, acc_ref, xb_ref, *, alpha):
    j = pl.program_id(1)

    @pl.when(j == 0)
    def _():
        xb_ref[...] = x_ref[...].astype(BF16)
        acc_ref[...] = jnp.zeros_like(acc_ref)

    xb = xb_ref[...]
    gate = jnp.dot(xb, wg_ref[...], preferred_element_type=F32)
    up = jnp.dot(xb, wu_ref[...], preferred_element_type=F32)
    h = (gate * jax.nn.sigmoid(gate) * up).astype(BF16)
    acc_ref[...] += jnp.dot(h, w2_ref[...], preferred_element_type=F32)

    @pl.when(j == pl.num_programs(1) - 1)
    def _():
        y = _layer_norm(alpha * x_ref[...] + 0.5 * acc_ref[...], g_ref[...], b_ref[...])
        o_ref[...] = y
        ob_ref[...] = y.astype(BF16)


def _ffn(x, w13, w2, g, b, alpha, tm=512, tf=512):
    T, D = x.shape
    F = w2.shape[0]
    tm, tf = min(tm, T), min(tf, F)
    nf = F // tf
    row = lambda i, j: (i, 0)
    vec = lambda i, j: (0, 0)
    return pl.pallas_call(
        functools.partial(_ffn_kernel, alpha=alpha),
        grid=(T // tm, nf),
        in_specs=[pl.BlockSpec((tm, D), row),
                  pl.BlockSpec((D, tf), lambda i, j: (0, j)),
                  pl.BlockSpec((D, tf), lambda i, j: (0, j + nf)),
                  pl.BlockSpec((tf, D), lambda i, j: (j, 0)),
                  pl.BlockSpec((1, D), vec),
                  pl.BlockSpec((1, D), vec)],
        out_specs=[pl.BlockSpec((tm, D), row), pl.BlockSpec((tm, D), row)],
        out_shape=[jax.ShapeDtypeStruct((T, D), F32), jax.ShapeDtypeStruct((T, D), BF16)],
        scratch_shapes=[pltpu.VMEM((tm, D), F32), pltpu.VMEM((tm, D), BF16)],
        compiler_params=_params(("parallel", "arbitrary")),
        name="ffn",
    )(x, w13, w13, w2, g, b)


def _mm_kernel(x_ref, w_ref, o_ref):
    o_ref[...] = jnp.dot(x_ref[...], w_ref[...], preferred_element_type=F32).astype(o_ref.dtype)


def _matmul(x, w, out_dtype, tm=512, tn=512):
    T, K = x.shape
    N = w.shape[1]
    tm = min(tm, T)
    tn = next(c for c in (tn, 384, 256, LANES) if N % c == 0)
    return pl.pallas_call(
        _mm_kernel,
        grid=(T // tm, N // tn),
        in_specs=[pl.BlockSpec((tm, K), lambda i, j: (i, 0)),
                  pl.BlockSpec((K, tn), lambda i, j: (0, j))],
        out_specs=pl.BlockSpec((tm, tn), lambda i, j: (i, j)),
        out_shape=jax.ShapeDtypeStruct((T, N), out_dtype),
        compiler_params=_params(("parallel", "arbitrary")),
        name="proj",
    )(x, w)


def _gate_kernel(x_ref, w_ref, bias_ref, tri_ref, c_ref, carry_ref):
    @pl.when(pl.program_id(1) == 0)
    def _():
        carry_ref[...] = jnp.zeros_like(carry_ref)

    z = jnp.dot(x_ref[0], w_ref[...], preferred_element_type=F32) + bias_ref[...]
    logf = jnp.minimum(z, 0.0) - jnp.log(1.0 + jnp.exp(-jnp.abs(z)))
    c = _split_dot_rhs(tri_ref[...], logf, 3) + carry_ref[...]
    c_ref[0] = c
    ts = c.shape[0]
    carry_ref[...] = c[ts - 1:ts, :]


def _gate_cumsum(xb, w_gate, bias, ts=512):
    B, S, D = xb.shape
    ts = min(ts, S)
    tri = jnp.tril(jnp.ones((ts, ts), F32)).astype(BF16)
    return pl.pallas_call(
        _gate_kernel,
        grid=(B, S // ts),
        in_specs=[pl.BlockSpec((1, ts, D), lambda b, s: (b, s, 0)),
                  pl.BlockSpec((D, LANES), lambda b, s: (0, 0)),
                  pl.BlockSpec((1, LANES), lambda b, s: (0, 0)),
                  pl.BlockSpec((ts, ts), lambda b, s: (0, 0))],
        out_specs=pl.BlockSpec((1, ts, LANES), lambda b, s: (b, s, 0)),
        out_shape=jax.ShapeDtypeStruct((B, S, LANES), F32),
        scratch_shapes=[pltpu.VMEM((1, LANES), F32)],
        compiler_params=_params(("parallel", "arbitrary")),
        name="fox_gate",
    )(xb, w_gate, bias, tri)


def _fox_kernel(q_ref, k_ref, v_ref, cc_ref, cr_ref, o_ref, *, tq, scale):
    p = pl.program_id(1)
    i = pl.program_id(2)
    q = q_ref[0]
    cc = cc_ref[0]
    lane = lax.broadcasted_iota(jnp.int32, (tq, LANES), 1)
    row = lax.broadcasted_iota(jnp.int32, (tq, tq), 0)
    col = lax.broadcasted_iota(jnp.int32, (tq, tq), 1)
    causal = col <= row
    outs = []
    for e in range(2):
        in_head = (lane < HEAD_DIM) if e == 0 else (lane >= HEAD_DIM)
        qe = jnp.where(in_head, q, jnp.zeros_like(q)) * scale
        c_col = jnp.sum(jnp.where(lane == 2 * p + e, cc, 0.0), axis=1, keepdims=True)

        def step(j, carry, diagonal, qe=qe, c_col=c_col, e=e):
            m, l, acc = carry
            off = pl.multiple_of(j * tq, tq)
            kb = k_ref[0, pl.ds(off, tq), :]
            vb = v_ref[0, pl.ds(off, tq), :]
            c_row = cr_ref[0, 0, pl.ds(e, 1), pl.ds(off, tq)]
            s = _dot_nt(qe, kb) + c_col - c_row
            if diagonal:
                s = jnp.where(causal, s, NEG_INF)
            m_new = jnp.maximum(m, jnp.max(s, axis=1, keepdims=True))
            rescale = jnp.exp(m - m_new)
            pexp = jnp.exp(s - m_new)
            l = rescale * l + jnp.sum(pexp, axis=1, keepdims=True)
            acc = rescale * acc + jnp.dot(pexp.astype(BF16), vb, preferred_element_type=F32)
            return m_new, l, acc

        init = (jnp.full((tq, 1), NEG_INF, F32), jnp.zeros((tq, 1), F32), jnp.zeros((tq, LANES), F32))
        carry = lax.fori_loop(0, i, functools.partial(step, diagonal=False), init)
        _, l, acc = step(i, carry, True)
        outs.append(acc / l)
    o_ref[0] = jnp.where(lane < HEAD_DIM, outs[0], outs[1])


def _fox_attention(qkv, c_col, c_row, tq=256):
    B, S, W3 = qkv.shape
    W = W3 // 3
    npair = W // LANES
    tq = min(tq, S)
    return pl.pallas_call(
        functools.partial(_fox_kernel, tq=tq, scale=HEAD_DIM ** -0.5),
        grid=(B, npair, S // tq),
        in_specs=[pl.BlockSpec((1, tq, LANES), lambda b, p, i: (b, i, p)),
                  pl.BlockSpec((1, S, LANES), lambda b, p, i: (b, 0, npair + p)),
                  pl.BlockSpec((1, S, LANES), lambda b, p, i: (b, 0, 2 * npair + p)),
                  pl.BlockSpec((1, tq, LANES), lambda b, p, i: (b, i, 0)),
                  pl.BlockSpec((1, 1, 2, S), lambda b, p, i: (b, p, 0, 0))],
        out_specs=pl.BlockSpec((1, tq, LANES), lambda b, p, i: (b, i, p)),
        out_shape=jax.ShapeDtypeStruct((B, S, W), F32),
        compiler_params=_params(("parallel", "parallel", "arbitrary")),
        name="fox_attn",
    )(qkv, qkv, qkv, c_col, c_row)


def _rwkv_prep_kernel(u_ref, up_ref, mu_ref, w0_ref, a0_ref, kk_ref, ka_ref, rk_ref, wup_ref, aup_ref, gup_ref,
                      seg_ref, segt_ref,
                      r_o, lw_o, k_o, v_o, kkn_o, b_o, g_o, bonus_o, *, tm, width):
    i = pl.program_id(1)
    first_row = lax.broadcasted_iota(jnp.int32, (tm, 1), 0) == 0
    has_prev = i > 0

    def shifted(lo, hi):
        u = u_ref[0, :, lo:hi]
        prev_last = jnp.where(has_prev, up_ref[0, 7:8, lo:hi], 0.0)
        u_prev = jnp.where(first_row, prev_last, pltpu.roll(u, 1, 0))
        return u + mu_ref[:, lo:hi] * (u_prev - u)

    def head_sum(t):
        per_head = _split_dot(t, seg_ref[...], 3)
        return _split_dot(per_head, segt_ref[...], 3)

    W = width
    r = shifted(0, W)
    k = shifted(W, 2 * W)
    v = shifted(2 * W, 3 * W)
    wd = shifted(3 * W, 3 * W + LANES)
    ad = shifted(3 * W + LANES, 3 * W + 2 * LANES)
    gd = shifted(3 * W + 2 * LANES, 3 * W + 4 * LANES)

    wlin = w0_ref[...] + jnp.dot(jnp.tanh(wd).astype(BF16), wup_ref[...], preferred_element_type=F32)
    z = -wlin
    w = -(jnp.maximum(z, 0.0) + jnp.log(1.0 + jnp.exp(-jnp.abs(z)))) - 0.5
    lw_o[0] = -jnp.exp(w)
    a = jax.nn.sigmoid(a0_ref[...] + jnp.dot(ad.astype(BF16), aup_ref[...], preferred_element_type=F32))
    g_o[0] = jnp.dot(jax.nn.sigmoid(gd).astype(BF16), gup_ref[...], preferred_element_type=F32)
    kk = k * kk_ref[...]
    kkn = kk / jnp.maximum(jnp.sqrt(head_sum(kk * kk)), 1e-12)
    k2 = k * (1.0 + (a - 1.0) * ka_ref[...])
    r_o[0] = r
    k_o[0] = k2
    v_o[0] = v
    kkn_o[0] = kkn
    b_o[0] = a * kkn
    bonus_o[0] = head_sum(r * k2 * rk_ref[...]) * v


def _rwkv_prep(u, mu, w0, a0, k_k, k_a, r_k, w_up, a_up, g_up, tm=256):
    B, S, UW = u.shape
    W = w0.shape[1]
    tm = min(tm, S)
    nh = W // HEAD_DIM
    head_of = jnp.arange(W) // HEAD_DIM
    seg = (head_of[:, None] == jnp.arange(LANES)[None, :]).astype(BF16)
    segt = seg.T
    del nh
    blk = lambda b, i: (b, i, 0)
    vec = lambda b, i: (0, 0)
    out = jax.ShapeDtypeStruct((B, S, W), F32)
    return pl.pallas_call(
        functools.partial(_rwkv_prep_kernel, tm=tm, width=W),
        grid=(B, S // tm),
        in_specs=[pl.BlockSpec((1, tm, UW), blk),
                  pl.BlockSpec((1, 8, UW), lambda b, i: (b, jnp.maximum(i * (tm // 8) - 1, 0), 0)),
                  pl.BlockSpec((1, UW), vec),
                  pl.BlockSpec((1, W), vec), pl.BlockSpec((1, W), vec), pl.BlockSpec((1, W), vec),
                  pl.BlockSpec((1, W), vec), pl.BlockSpec((1, W), vec),
                  pl.BlockSpec((LANES, W), vec), pl.BlockSpec((LANES, W), vec), pl.BlockSpec((2 * LANES, W), vec),
                  pl.BlockSpec((W, LANES), vec), pl.BlockSpec((LANES, W), vec)],
        out_specs=[pl.BlockSpec((1, tm, W), blk)] * 8,
        out_shape=[out] * 8,
        compiler_params=_params(("parallel", "arbitrary")),
        name="rwkv_prep",
    )(u, u, mu, w0, a0, k_k, k_a, r_k, w_up, a_up, g_up, seg, segt)


def _rwkv_scan_kernel(r_ref, lw_ref, k_ref, v_ref, kkn_ref, b_ref, g_ref, bonus_ref, gng_ref, gnb_ref,
                      tri_ref, ones_ref, o_ref, s_ref, *, nchunk):
    C = CHUNK

    @pl.when(pl.program_id(2) == 0)
    def _():
        s_ref[...] = jnp.zeros_like(s_ref)

    t_i = lax.broadcasted_iota(jnp.int32, (C, LANES), 0)
    lane = lax.broadcasted_iota(jnp.int32, (C, LANES), 1)
    s_i = lane & (HEAD_DIM - 1)
    low = lane < HEAD_DIM
    strict = s_i < t_i
    incl = s_i <= t_i
    eye = (s_i == t_i).astype(F32)
    levels = []
    k_ = 0
    while (1 << k_) < C:
        levels.append(((t_i >> k_) == ((s_i >> k_) + 1)) & ((t_i >> (k_ + 1)) == (s_i >> (k_ + 1))))
        k_ += 1
    blk_r = lax.broadcasted_iota(jnp.int32, (LANES, LANES), 0) // HEAD_DIM
    blk_c = lax.broadcasted_iota(jnp.int32, (LANES, LANES), 1) // HEAD_DIM
    same_head = blk_r == blk_c

    def block_diag(y):
        zero = jnp.zeros_like(y)
        return jnp.concatenate([jnp.where(low, y, zero), jnp.where(low, zero, y)], axis=0)

    def head_dot(x, y):
        return jnp.dot(x.astype(BF16), block_diag(y.astype(BF16)), preferred_element_type=F32)

    tri = tri_ref[...]
    ones_bd = ones_ref[...]
    gn_g = gng_ref[...]
    gn_b = gnb_ref[...]

    def chunk(c, carry):
        off = pl.multiple_of(c * C, C)
        sl = pl.ds(off, C)
        r = r_ref[0, sl, :]
        lw = lw_ref[0, sl, :]
        k = k_ref[0, sl, :]
        v = v_ref[0, sl, :]
        kkn = kkn_ref[0, sl, :]
        b = b_ref[0, sl, :]
        s0 = s_ref[...]

        cum = _split_dot_rhs(tri, lw, 3)
        cum_last = cum[C - 1:C, :]
        g_t = jnp.exp(cum)
        g_inv = jnp.exp(-cum)
        g_prev = jnp.exp(cum - lw)
        g_tail = jnp.exp(cum_last - cum)
        a_t = (-kkn * g_prev).astype(BF16)
        r_t = (r * g_t).astype(BF16)
        k_t = (k * g_inv).astype(BF16)
        b_t = (b * g_inv).astype(BF16)
        lhs = jnp.concatenate([a_t, r_t], axis=0)
        rhs = jnp.concatenate([block_diag(k_t), block_diag(b_t)], axis=0)
        amat = _dot_nt(lhs, rhs)
        a_ak = jnp.where(strict, amat[:C, :LANES], 0.0)
        n_ab = jnp.where(strict, amat[:C, LANES:], 0.0)
        a_rk = jnp.where(incl, amat[C:, :LANES], 0.0)
        a_rb = jnp.where(incl, amat[C:, LANES:], 0.0)

        inv = eye + jnp.where(levels[0], n_ab, 0.0)
        for lvl in levels[1:]:
            inv = inv + head_dot(head_dot(inv, jnp.where(lvl, n_ab, 0.0)), inv)

        s_proj = _dot_nt(lhs, s0.astype(BF16))
        vb = block_diag(v.astype(BF16))
        rhs_u = s_proj[:C] + jnp.dot(a_ak.astype(BF16), vb, preferred_element_type=F32)
        u = head_dot(inv, rhs_u)
        ub = block_diag(u.astype(BF16))
        y = s_proj[C:] + jnp.dot(jnp.concatenate([a_rk, a_rb], axis=1).astype(BF16),
                                 jnp.concatenate([vb, ub], axis=0), preferred_element_type=F32)

        vu = jnp.concatenate([v, u], axis=0).astype(BF16)
        kb = jnp.concatenate([k * g_tail, b * g_tail], axis=0).astype(BF16)
        s_ref[...] = s0 * jnp.exp(cum_last) + jnp.where(same_head, _dot_tn(vu, kb), 0.0)

        mean = _split_dot(y, ones_bd, 2) * (1.0 / HEAD_DIM)
        yc = y - mean
        var = _split_dot(yc * yc, ones_bd, 2) * (1.0 / HEAD_DIM)
        yn = yc * lax.rsqrt(var + GN_EPS) * gn_g + gn_b
        o_ref[0, sl, :] = (yn + bonus_ref[0, sl, :]) * g_ref[0, sl, :]
        return carry

    lax.fori_loop(0, nchunk, chunk, 0)


def _rwkv_scan(r, lw, k, v, kkn, b, g, bonus, gn_g, gn_b, tc=256):
    B, S, W = r.shape
    tc = min(tc, S)
    C = CHUNK
    tri = jnp.tril(jnp.ones((C, C), F32)).astype(BF16)
    blk_id = jnp.arange(LANES) // HEAD_DIM
    ones_bd = (blk_id[:, None] == blk_id[None, :]).astype(BF16)
    seq = pl.BlockSpec((1, tc, LANES), lambda bb, p, t: (bb, t, p))
    vec = pl.BlockSpec((1, LANES), lambda bb, p, t: (0, p))
    return pl.pallas_call(
        functools.partial(_rwkv_scan_kernel, nchunk=tc // C),
        grid=(B, W // LANES, S // tc),
        in_specs=[seq] * 8 + [vec, vec,
                              pl.BlockSpec((C, C), lambda bb, p, t: (0, 0)),
                              pl.BlockSpec((LANES, LANES), lambda bb, p, t: (0, 0))],
        out_specs=seq,
        out_shape=jax.ShapeDtypeStruct((B, S, W), F32),
        scratch_shapes=[pltpu.VMEM((LANES, LANES), F32)],
        compiler_params=_params(("parallel", "parallel", "arbitrary")),
        name="rwkv_scan",
    )(r, lw, k, v, kkn, b, g, bonus, gn_g, gn_b, tri, ones_bd)


def _mix_kernel(fo_ref, ro_ref, x_ref, wf_ref, wr_ref, fg_ref, g_ref, b_ref, o_ref, *, alpha):
    fo = fo_ref[...]
    fo = fo * lax.rsqrt(jnp.mean(fo * fo, axis=-1, keepdims=True) + FOX_EPS) * fg_ref[...]
    mix = jnp.dot(fo.astype(BF16), wf_ref[...], preferred_element_type=F32)
    mix = mix + jnp.dot(ro_ref[...].astype(BF16), wr_ref[...], preferred_element_type=F32)
    o_ref[...] = _layer_norm(alpha * x_ref[...] + mix, g_ref[...], b_ref[...])


def _mix(fo, ro, x, w_fox, w_rwkv, fox_g, g, b, alpha, tm=256):
    T, D = x.shape
    WF, WR = fo.shape[1], ro.shape[1]
    tm = min(tm, T)
    row = lambda i: (i, 0)
    vec = lambda i: (0, 0)
    return pl.pallas_call(
        functools.partial(_mix_kernel, alpha=alpha),
        grid=(T // tm,),
        in_specs=[pl.BlockSpec((tm, WF), row), pl.BlockSpec((tm, WR), row), pl.BlockSpec((tm, D), row),
                  pl.BlockSpec((WF, D), vec), pl.BlockSpec((WR, D), vec),
                  pl.BlockSpec((1, WF), vec), pl.BlockSpec((1, D), vec), pl.BlockSpec((1, D), vec)],
        out_specs=pl.BlockSpec((tm, D), row),
        out_shape=jax.ShapeDtypeStruct((T, D), F32),
        compiler_params=_params(("parallel",)),
        name="mix_out",
    )(fo, ro, x, w_fox, w_rwkv, fox_g, g, b)


def _xattn_kernel(x_ref, wq_ref, kv_k_ref, kv_v_ref, wo_ref, g_ref, b_ref, o_ref, *, alpha, heads):
    x = x_ref[0]
    D = x.shape[1]
    dh = D // heads
    q = jnp.dot(x.astype(BF16), wq_ref[...], preferred_element_type=F32).astype(BF16)
    outs = []
    for h in range(heads):
        cols = slice(h * dh, (h + 1) * dh)
        s = _dot_nt(q[:, cols], kv_k_ref[0, :, cols]) * (dh ** -0.5)
        s = s - jnp.max(s, axis=1, keepdims=True)
        e = jnp.exp(s)
        p = e / jnp.sum(e, axis=1, keepdims=True)
        outs.append(jnp.dot(p.astype(BF16), kv_v_ref[0, :, cols], preferred_element_type=F32).astype(BF16))
    o = jnp.concatenate(outs, axis=1)
    xa = jnp.dot(o, wo_ref[...], preferred_element_type=F32)
    o_ref[0] = _layer_norm(alpha * x + xa, g_ref[...], b_ref[...])


def _xattn(x, kv, wq, wo, g, b, alpha, tm=256):
    B, S, D = x.shape
    M = kv.shape[1]
    tm = min(tm, S)
    vec = lambda bb, i: (0, 0)
    return pl.pallas_call(
        functools.partial(_xattn_kernel, alpha=alpha, heads=XA_HEADS),
        grid=(B, S // tm),
        in_specs=[pl.BlockSpec((1, tm, D), lambda bb, i: (bb, i, 0)),
                  pl.BlockSpec((D, D), vec),
                  pl.BlockSpec((1, M, D), lambda bb, i: (bb, 0, 0)),
                  pl.BlockSpec((1, M, D), lambda bb, i: (bb, 0, 1)),
                  pl.BlockSpec((D, D), vec),
                  pl.BlockSpec((1, D), vec), pl.BlockSpec((1, D), vec)],
        out_specs=pl.BlockSpec((1, tm, D), lambda bb, i: (bb, i, 0)),
        out_shape=jax.ShapeDtypeStruct((B, S, D), F32),
        compiler_params=_params(("parallel", "arbitrary")),
        name="xattn",
    )(x, wq, kv, kv, wo, g, b)


def _pad_cols(w, n):
    return jnp.pad(w, ((0, 0), (0, n - w.shape[1])))


def _pad_rows(w, n):
    return jnp.pad(w, ((0, n - w.shape[0]), (0, 0)))


def kernel(x, mem, ln_g, ln_b, ffn1_w13, ffn1_w2, w_in, fox_f_bias, fox_out_g, rwkv_mu, rwkv_w0, rwkv_w_up, rwkv_a0, rwkv_a_up, rwkv_g_up, rwkv_k_k, rwkv_k_a, rwkv_r_k, rwkv_gn_g, rwkv_gn_b, w_mix_out, xa_wq, xa_wkv, xa_wo, ffn2_w13, ffn2_w2):
    B, S, D = x.shape
    depth = ln_g.shape[0]
    M = mem.shape[1]
    alpha = (2.0 * depth) ** 0.25
    fox_heads = fox_f_bias.shape[1]
    WF = fox_heads * HEAD_DIM
    WR = rwkv_w0.shape[1]
    decay_lora, aaa_lora, gate_lora = rwkv_w_up.shape[1], rwkv_a_up.shape[1], rwkv_g_up.shape[1]
    c1 = 3 * WF
    c2 = c1 + fox_heads
    T = B * S
    row = lambda t: t.reshape(1, -1)

    h = x.reshape(T, D)
    mem_b = mem.reshape(B * M, D).astype(BF16)
    for l in range(depth):
        h, hb = _ffn(h, ffn1_w13[l].astype(BF16), ffn1_w2[l].astype(BF16), row(ln_g[l, 0]), row(ln_b[l, 0]), alpha)

        wl = w_in[l]
        w_fox = wl[:, :c1].astype(BF16)
        w_gate = _pad_cols(wl[:, c1:c2], LANES).astype(BF16)
        o = c2 + 3 * WR
        w_rw = jnp.concatenate([wl[:, c2:o],
                                _pad_cols(wl[:, o:o + decay_lora], LANES),
                                _pad_cols(wl[:, o + decay_lora:o + decay_lora + aaa_lora], LANES),
                                _pad_cols(wl[:, o + decay_lora + aaa_lora:], 2 * LANES)], axis=1).astype(BF16)
        mu = rwkv_mu[l]
        o = 3 * WR
        mu_p = jnp.concatenate([mu[:o],
                                jnp.pad(mu[o:o + decay_lora], (0, LANES - decay_lora)),
                                jnp.pad(mu[o + decay_lora:o + decay_lora + aaa_lora], (0, LANES - aaa_lora)),
                                jnp.pad(mu[o + decay_lora + aaa_lora:], (0, 2 * LANES - gate_lora))])

        qkv = _matmul(hb, w_fox, BF16).reshape(B, S, c1)
        u = _matmul(hb, w_rw, F32).reshape(B, S, -1)
        bias = jnp.pad(fox_f_bias[l], (0, LANES - fox_heads)).reshape(1, LANES)
        c_col = _gate_cumsum(hb.reshape(B, S, D), w_gate, bias)
        c_row = jnp.swapaxes(c_col[:, :, :fox_heads], 1, 2).reshape(B, fox_heads // 2, 2, S)
        fo = _fox_attention(qkv, c_col, c_row)

        r, lw, k2, v, kkn, bb, g, bonus = _rwkv_prep(
            u, row(mu_p), row(rwkv_w0[l]), row(rwkv_a0[l]), row(rwkv_k_k[l]), row(rwkv_k_a[l]),
            row(rwkv_r_k[l]),
            _pad_rows(rwkv_w_up[l], LANES).astype(BF16), _pad_rows(rwkv_a_up[l], LANES).astype(BF16),
            _pad_rows(rwkv_g_up[l], 2 * LANES).astype(BF16))
        ro = _rwkv_scan(r, lw, k2, v, kkn, bb, g, bonus, row(rwkv_gn_g[l]), row(rwkv_gn_b[l]))

        wm = w_mix_out[l].astype(BF16)
        h = _mix(fo.reshape(T, WF), ro.reshape(T, WR), h, wm[:WF], wm[WF:], row(fox_out_g[l]),
                 row(ln_g[l, 1]), row(ln_b[l, 1]), alpha)

        kv = _matmul(mem_b, xa_wkv[l].astype(BF16), BF16).reshape(B, M, 2 * D)
        h = _xattn(h.reshape(B, S, D), kv, xa_wq[l].astype(BF16), xa_wo[l].astype(BF16),
                   row(ln_g[l, 2]), row(ln_b[l, 2]), alpha).reshape(T, D)

        h, _ = _ffn(h, ffn2_w13[l].astype(BF16), ffn2_w2[l].astype(BF16), row(ln_g[l, 3]), row(ln_b[l, 3]), alpha)
    return h.reshape(B, S, D)
```

```python
import functools

import jax
import jax.numpy as jnp
from jax import lax
from jax.experimental import pallas as pl
from jax.experimental.pallas import tpu as pltpu

HEAD_DIM = 64
XA_HEADS = 4
LN_EPS = 1e-5
GN_EPS = 64e-5
FOX_EPS = 1e-6
NEG_INF = -1e30
LANES = 128
CHUNK = 64
VMEM_LIMIT = 56 * 1024 * 1024

F32 = jnp.float32
BF16 = jnp.bfloat16


def _params(semantics):
    return pltpu.CompilerParams(dimension_semantics=semantics, vmem_limit_bytes=VMEM_LIMIT)


def _layer_norm(z, g, b):
    mu = jnp.mean(z, axis=-1, keepdims=True)
    zc = z - mu
    var = jnp.mean(zc * zc, axis=-1, keepdims=True)
    return zc * lax.rsqrt(var + LN_EPS) * g + b


def _split_dot(a, b01, terms):
    acc = None
    rem = a
    for _ in range(terms):
        part = rem.astype(BF16)
        d = jnp.dot(part, b01, preferred_element_type=F32)
        acc = d if acc is None else acc + d
        rem = rem - part.astype(F32)
    return acc


def _split_dot_rhs(a01, b, terms):
    acc = None
    rem = b
    for _ in range(terms):
        part = rem.astype(BF16)
        d = jnp.dot(a01, part, preferred_element_type=F32)
        acc = d if acc is None else acc + d
        rem = rem - part.astype(F32)
    return acc


def _dot_nt(a, b):
    return lax.dot_general(a, b, (((1,), (1,)), ((), ())), preferred_element_type=F32)


def _dot_tn(a, b):
    return lax.dot_general(a, b, (((0,), (0,)), ((), ())), preferred_element_type=F32)


def _ffn_kernel(x_ref, wg_ref, wu_ref, w2_ref, g_ref, b_ref, o_ref, ob_ref, acc_ref, xb_ref, *, alpha):
    j = pl.program_id(1)

    @pl.when(j == 0)
    def _():
        xb_ref[...] = x_ref[...].astype(BF16)
        acc_ref[...] = jnp.zeros_like(acc_ref)

    xb = xb_ref[...]
    gate = jnp.dot(xb, wg_ref[...], preferred_element_type=F32)
    up = jnp.dot(xb, wu_ref[...], preferred_element_type=F32)
    h = (gate * jax.nn.sigmoid(gate) * up).astype(BF16)
    acc_ref[...] += jnp.dot(h, w2_ref[...], preferred_element_type=F32)

    @pl.when(j == pl.num_programs(1) - 1)
    def _():
        y = _layer_norm(alpha * x_ref[...] + 0.5 * acc_ref[...], g_ref[...], b_ref[...])
        o_ref[...] = y
        ob_ref[...] = y.astype(BF16)


def _ffn(x, w13, w2, g, b, alpha, tm=512, tf=512):
    T, D = x.shape
    F = w2.shape[0]
    tm, tf = min(tm, T), min(tf, F)
    nf = F // tf
    row = lambda i, j: (i, 0)
    vec = lambda i, j: (0, 0)
    return pl.pallas_call(
        functools.partial(_ffn_kernel, alpha=alpha),
        grid=(T // tm, nf),
        in_specs=[pl.BlockSpec((tm, D), row),
                  pl.BlockSpec((D, tf), lambda i, j: (0, j)),
                  pl.BlockSpec((D, tf), lambda i, j: (0, j + nf)),
                  pl.BlockSpec((tf, D), lambda i, j: (j, 0)),
                  pl.BlockSpec((1, D), vec),
                  pl.BlockSpec((1, D), vec)],
        out_specs=[pl.BlockSpec((tm, D), row), pl.BlockSpec((tm, D), row)],
        out_shape=[jax.ShapeDtypeStruct((T, D), F32), jax.ShapeDtypeStruct((T, D), BF16)],
        scratch_shapes=[pltpu.VMEM((tm, D), F32), pltpu.VMEM((tm, D), BF16)],
        compiler_params=_params(("parallel", "arbitrary")),
        name="ffn",
    )(x, w13, w13, w2, g, b)


def _mm_kernel(x_ref, w_ref, o_ref):
    o_ref[...] = jnp.dot(x_ref[...], w_ref[...], preferred_element_type=F32).astype(o_ref.dtype)


def _matmul(x, w, out_dtype, tm=512, tn=512):
    T, K = x.shape
    N = w.shape[1]
    tm = min(tm, T)
    tn = next(c for c in (tn, 384, 256, LANES) if N % c == 0)
    return pl.pallas_call(
        _mm_kernel,
        grid=(T // tm, N // tn),
        in_specs=[pl.BlockSpec((tm, K), lambda i, j: (i, 0)),
                  pl.BlockSpec((K, tn), lambda i, j: (0, j))],
        out_specs=pl.BlockSpec((tm, tn), lambda i, j: (i, j)),
        out_shape=jax.ShapeDtypeStruct((T, N), out_dtype),
        compiler_params=_params(("parallel", "arbitrary")),
        name="proj",
    )(x, w)


def _gate_kernel(x_ref, w_ref, bias_ref, tri_ref, c_ref, carry_ref):
    @pl.when(pl.program_id(1) == 0)
    def _():
        carry_ref[...] = jnp.zeros_like(carry_ref)

    z = jnp.dot(x_ref[0], w_ref[...], preferred_element_type=F32) + bias_ref[...]
    logf = jnp.minimum(z, 0.0) - jnp.log(1.0 + jnp.exp(-jnp.abs(z)))
    c = _split_dot_rhs(tri_ref[...], logf, 3) + carry_ref[...]
    c_ref[0] = c
    ts = c.shape[0]
    carry_ref[...] = c[ts - 1:ts, :]


def _gate_cumsum(xb, w_gate, bias, ts=512):
    B, S, D = xb.shape
    ts = min(ts, S)
    tri = jnp.tril(jnp.ones((ts, ts), F32)).astype(BF16)
    return pl.pallas_call(
        _gate_kernel,
        grid=(B, S // ts),
        in_specs=[pl.BlockSpec((1, ts, D), lambda b, s: (b, s, 0)),
                  pl.BlockSpec((D, LANES), lambda b, s: (0, 0)),
                  pl.BlockSpec((1, LANES), lambda b, s: (0, 0)),
                  pl.BlockSpec((ts, ts), lambda b, s: (0, 0))],
        out_specs=pl.BlockSpec((1, ts, LANES), lambda b, s: (b, s, 0)),
        out_shape=jax.ShapeDtypeStruct((B, S, LANES), F32),
        scratch_shapes=[pltpu.VMEM((1, LANES), F32)],
        compiler_params=_params(("parallel", "arbitrary")),
        name="fox_gate",
    )(xb, w_gate, bias, tri)


def _fox_kernel(q_ref, k_ref, v_ref, cc_ref, o_ref, kp_ref, qb_ref, vt_ref, *, tq, tk, prep_rows, scale):
    p = pl.program_id(1)
    i = pl.program_id(2)
    S = k_ref.shape[1]

    @pl.when(i == 0)
    def _():
        rows = lax.broadcasted_iota(jnp.int32, (LANES, LANES), 0)
        cols = lax.broadcasted_iota(jnp.int32, (LANES, LANES), 1)
        ident = (rows == cols).astype(BF16)
        lane = lax.broadcasted_iota(jnp.int32, (prep_rows, LANES), 1)

        def prep(n, carry):
            sl = pl.ds(pl.multiple_of(n * prep_rows, prep_rows), prep_rows)
            kk = k_ref[0, sl, :]
            vv = v_ref[0, sl, :]
            cc = cc_ref[0, sl, :]
            c1 = cc.astype(BF16)
            rem = cc - c1.astype(F32)
            c2 = rem.astype(BF16)
            c3 = (rem - c2.astype(F32)).astype(BF16)
            for e in range(2):
                base = HEAD_DIM * (1 - e)
                in_head = (lane >= HEAD_DIM * e) & (lane < HEAD_DIM * (e + 1))

                def place(first, base=base, e=e):
                    out = None
                    for t, part in enumerate((c1, c2, c3)):
                        onehot = ((rows == 2 * p + e) & (cols == base + first + t)).astype(BF16)
                        d = jnp.dot(part, onehot, preferred_element_type=F32)
                        out = d if out is None else out + d
                    return out

                ones_k = ((lane >= base) & (lane < base + 3)).astype(F32)
                ones_q = ((lane >= base + 3) & (lane < base + 6)).astype(F32)
                kp_ref[e, sl, :] = jnp.where(in_head, kk, (ones_k - place(3)).astype(BF16))
                qb_ref[e, sl, :] = (place(0) + ones_q).astype(BF16)
                v_aug = jnp.where(in_head, vv, jnp.ones_like(vv))
                vt_ref[e, :, sl] = _dot_nt(ident, v_aug).astype(BF16)
            return carry

        lax.fori_loop(0, S // prep_rows, prep, 0)

    q = q_ref[0] * scale
    lane = lax.broadcasted_iota(jnp.int32, (tq, LANES), 1)
    key_pos = lax.broadcasted_iota(jnp.int32, (tq, tq), 0)
    qry_pos = lax.broadcasted_iota(jnp.int32, (tq, tq), 1)
    causal = key_pos <= qry_pos
    q_rows = pl.ds(pl.multiple_of(i * tq, tq), tq)
    heads = range(2)
    qp = [jnp.where((lane >= HEAD_DIM * e) & (lane < HEAD_DIM * (e + 1)), q, qb_ref[e, q_rows, :]) for e in heads]

    def step(carry, start, width, diagonal):
        kv_rows = pl.ds(start, width)
        s = [_dot_nt(kp_ref[e, kv_rows, :], qp[e]) for e in heads]
        if diagonal:
            s = [jnp.where(causal, s[e], NEG_INF) for e in heads]
        m_new = [jnp.maximum(carry[2 * e], jnp.max(s[e], axis=0, keepdims=True)) for e in heads]
        pexp = [jnp.exp(s[e] - m_new[e]).astype(BF16) for e in heads]
        acc = [jnp.exp(carry[2 * e] - m_new[e]) * carry[2 * e + 1]
               + jnp.dot(vt_ref[e, :, kv_rows], pexp[e], preferred_element_type=F32) for e in heads]
        return m_new[0], acc[0], m_new[1], acc[1]

    per_tile = tk // tq
    nfull = i // per_tile
    init = (jnp.full((1, tq), NEG_INF, F32), jnp.zeros((LANES, tq), F32)) * 2
    carry = lax.fori_loop(0, nfull, lambda j, c: step(c, pl.multiple_of(j * tk, tk), tk, False), init)
    if per_tile > 1:
        carry = lax.fori_loop(nfull * per_tile, i, lambda j, c: step(c, pl.multiple_of(j * tq, tq), tq, False), carry)
    carry = step(carry, pl.multiple_of(i * tq, tq), tq, True)
    outs = [carry[2 * e + 1] / carry[2 * e + 1][HEAD_DIM * (1 - e):HEAD_DIM * (1 - e) + 1, :] for e in heads]
    head_row = lax.broadcasted_iota(jnp.int32, (LANES, tq), 0)
    o_ref[0] = jnp.where(head_row < HEAD_DIM, outs[0], outs[1]).T


def _fox_attention(qkv, c, tq=256, tk=256):
    B, S, W3 = qkv.shape
    W = W3 // 3
    npair = W // LANES
    tq = min(tq, S)
    return pl.pallas_call(
        functools.partial(_fox_kernel, tq=tq, tk=max(tk, tq), prep_rows=min(512, S), scale=HEAD_DIM ** -0.5),
        grid=(B, npair, S // tq),
        in_specs=[pl.BlockSpec((1, tq, LANES), lambda b, p, i: (b, i, p)),
                  pl.BlockSpec((1, S, LANES), lambda b, p, i: (b, 0, npair + p)),
                  pl.BlockSpec((1, S, LANES), lambda b, p, i: (b, 0, 2 * npair + p)),
                  pl.BlockSpec((1, S, LANES), lambda b, p, i: (b, 0, 0))],
        out_specs=pl.BlockSpec((1, tq, LANES), lambda b, p, i: (b, i, p)),
        out_shape=jax.ShapeDtypeStruct((B, S, W), F32),
        scratch_shapes=[pltpu.VMEM((2, S, LANES), BF16), pltpu.VMEM((2, S, LANES), BF16),
                        pltpu.VMEM((2, LANES, S), BF16)],
        compiler_params=_params(("parallel", "parallel", "arbitrary")),
        name="fox_attn",
    )(qkv, qkv, qkv, c)


def _rwkv_prep_kernel(u_ref, up_ref, mu_ref, w0_ref, a0_ref, kk_ref, ka_ref, rk_ref, wup_ref, aup_ref, gup_ref,
                      seg_ref, segt_ref,
                      r_o, lw_o, k_o, v_o, kkn_o, b_o, g_o, bonus_o, *, tm, width):
    i = pl.program_id(1)
    first_row = lax.broadcasted_iota(jnp.int32, (tm, 1), 0) == 0
    has_prev = i > 0

    def shifted(lo, hi):
        u = u_ref[0, :, lo:hi]
        prev_last = jnp.where(has_prev, up_ref[0, 7:8, lo:hi], 0.0)
        u_prev = jnp.where(first_row, prev_last, pltpu.roll(u, 1, 0))
        return u + mu_ref[:, lo:hi] * (u_prev - u)

    def head_sum(t):
        per_head = _split_dot(t, seg_ref[...], 3)
        return _split_dot(per_head, segt_ref[...], 3)

    W = width
    r = shifted(0, W)
    k = shifted(W, 2 * W)
    v = shifted(2 * W, 3 * W)
    wd = shifted(3 * W, 3 * W + LANES)
    ad = shifted(3 * W + LANES, 3 * W + 2 * LANES)
    gd = shifted(3 * W + 2 * LANES, 3 * W + 4 * LANES)

    wlin = w0_ref[...] + jnp.dot(jnp.tanh(wd).astype(BF16), wup_ref[...], preferred_element_type=F32)
    z = -wlin
    w = -(jnp.maximum(z, 0.0) + jnp.log(1.0 + jnp.exp(-jnp.abs(z)))) - 0.5
    lw_o[0] = -jnp.exp(w)
    a = jax.nn.sigmoid(a0_ref[...] + jnp.dot(ad.astype(BF16), aup_ref[...], preferred_element_type=F32))
    g_o[0] = jnp.dot(jax.nn.sigmoid(gd).astype(BF16), gup_ref[...], preferred_element_type=F32)
    kk = k * kk_ref[...]
    kkn = kk / jnp.maximum(jnp.sqrt(head_sum(kk * kk)), 1e-12)
    k2 = k * (1.0 + (a - 1.0) * ka_ref[...])
    r_o[0] = r
    k_o[0] = k2
    v_o[0] = v
    kkn_o[0] = kkn
    b_o[0] = a * kkn
    bonus_o[0] = head_sum(r * k2 * rk_ref[...]) * v


def _rwkv_prep(u, mu, w0, a0, k_k, k_a, r_k, w_up, a_up, g_up, tm=256):
    B, S, UW = u.shape
    W = w0.shape[1]
    tm = min(tm, S)
    nh = W // HEAD_DIM
    head_of = jnp.arange(W) // HEAD_DIM
    seg = (head_of[:, None] == jnp.arange(LANES)[None, :]).astype(BF16)
    segt = seg.T
    del nh
    blk = lambda b, i: (b, i, 0)
    vec = lambda b, i: (0, 0)
    out = jax.ShapeDtypeStruct((B, S, W), F32)
    return pl.pallas_call(
        functools.partial(_rwkv_prep_kernel, tm=tm, width=W),
        grid=(B, S // tm),
        in_specs=[pl.BlockSpec((1, tm, UW), blk),
                  pl.BlockSpec((1, 8, UW), lambda b, i: (b, jnp.maximum(i * (tm // 8) - 1, 0), 0)),
                  pl.BlockSpec((1, UW), vec),
                  pl.BlockSpec((1, W), vec), pl.BlockSpec((1, W), vec), pl.BlockSpec((1, W), vec),
                  pl.BlockSpec((1, W), vec), pl.BlockSpec((1, W), vec),
                  pl.BlockSpec((LANES, W), vec), pl.BlockSpec((LANES, W), vec), pl.BlockSpec((2 * LANES, W), vec),
                  pl.BlockSpec((W, LANES), vec), pl.BlockSpec((LANES, W), vec)],
        out_specs=[pl.BlockSpec((1, tm, W), blk)] * 8,
        out_shape=[out] * 8,
        compiler_params=_params(("parallel", "arbitrary")),
        name="rwkv_prep",
    )(u, u, mu, w0, a0, k_k, k_a, r_k, w_up, a_up, g_up, seg, segt)


def _rwkv_scan_kernel(r_ref, lw_ref, k_ref, v_ref, kkn_ref, b_ref, g_ref, bonus_ref, gng_ref, gnb_ref,
                      tri_ref, ones_ref, o_ref, s_ref, *, nchunk, npair):
    C = CHUNK

    @pl.when(pl.program_id(2) == 0)
    def _():
        s_ref[...] = jnp.zeros_like(s_ref)

    t_i = lax.broadcasted_iota(jnp.int32, (C, LANES), 0)
    lane = lax.broadcasted_iota(jnp.int32, (C, LANES), 1)
    s_i = lane & (HEAD_DIM - 1)
    low = lane < HEAD_DIM
    strict = s_i < t_i
    incl = s_i <= t_i
    eye = (s_i == t_i).astype(F32)
    levels = []
    k_ = 0
    while (1 << k_) < C:
        levels.append(((t_i >> k_) == ((s_i >> k_) + 1)) & ((t_i >> (k_ + 1)) == (s_i >> (k_ + 1))))
        k_ += 1
    blk_r = lax.broadcasted_iota(jnp.int32, (LANES, LANES), 0) // HEAD_DIM
    blk_c = lax.broadcasted_iota(jnp.int32, (LANES, LANES), 1) // HEAD_DIM
    same_head = blk_r == blk_c

    def block_diag(y):
        zero = jnp.zeros_like(y)
        return jnp.concatenate([jnp.where(low, y, zero), jnp.where(low, zero, y)], axis=0)

    def head_dot(x, y):
        return jnp.dot(x.astype(BF16), block_diag(y.astype(BF16)), preferred_element_type=F32)

    tri = tri_ref[...]
    ones_bd = ones_ref[...]

    def chunk(c, carry):
        sl = pl.ds(pl.multiple_of(c * C, C), C)
        pairs = range(npair)
        col = lambda t, p: t[:, p * LANES:(p + 1) * LANES]
        r = r_ref[0, sl, :]
        lw = lw_ref[0, sl, :]
        k = k_ref[0, sl, :]
        v = v_ref[0, sl, :]
        kkn = kkn_ref[0, sl, :]
        b = b_ref[0, sl, :]

        cum = _split_dot_rhs(tri, lw, 3)
        cum_last = cum[C - 1:C, :]
        g_inv = jnp.exp(-cum)
        g_tail = jnp.exp(cum_last - cum)
        a_t = (-kkn * jnp.exp(cum - lw)).astype(BF16)
        r_t = (r * jnp.exp(cum)).astype(BF16)
        k_t = (k * g_inv).astype(BF16)
        b_t = (b * g_inv).astype(BF16)
        k_tail = (k * g_tail).astype(BF16)
        b_tail = (b * g_tail).astype(BF16)
        v16 = v.astype(BF16)
        decay_last = jnp.exp(cum_last)

        s0 = [s_ref[p] for p in pairs]
        lhs = [jnp.concatenate([col(a_t, p), col(r_t, p)], axis=0) for p in pairs]
        amat = [_dot_nt(lhs[p], jnp.concatenate([block_diag(col(k_t, p)), block_diag(col(b_t, p))], axis=0))
                for p in pairs]
        s_proj = [_dot_nt(lhs[p], s0[p].astype(BF16)) for p in pairs]
        n_ab = [jnp.where(strict, amat[p][:C, LANES:], 0.0) for p in pairs]
        inv = [eye + jnp.where(levels[0], n_ab[p], 0.0) for p in pairs]
        for lvl in levels[1:]:
            x = [head_dot(inv[p], jnp.where(lvl, n_ab[p], 0.0)) for p in pairs]
            inv = [inv[p] + head_dot(x[p], inv[p]) for p in pairs]

        vbd = [block_diag(col(v16, p)) for p in pairs]
        rhs_u = [s_proj[p][:C] + jnp.dot(jnp.where(strict, amat[p][:C, :LANES], 0.0).astype(BF16), vbd[p],
                                         preferred_element_type=F32) for p in pairs]
        u = [head_dot(inv[p], rhs_u[p]) for p in pairs]
        u16 = [u[p].astype(BF16) for p in pairs]
        y = [s_proj[p][C:] + jnp.dot(jnp.where(jnp.concatenate([incl, incl], axis=1), amat[p][C:], 0.0).astype(BF16),
                                     jnp.concatenate([vbd[p], block_diag(u16[p])], axis=0),
                                     preferred_element_type=F32) for p in pairs]
        s_upd = [_dot_tn(jnp.concatenate([col(v16, p), u16[p]], axis=0),
                         jnp.concatenate([col(k_tail, p), col(b_tail, p)], axis=0)) for p in pairs]
        s_ref[...] = jnp.stack([s0[p] * col(decay_last, p) + jnp.where(same_head, s_upd[p], 0.0) for p in pairs],
                               axis=0)

        mean = [_split_dot(y[p], ones_bd, 2) * (1.0 / HEAD_DIM) for p in pairs]
        yc = [y[p] - mean[p] for p in pairs]
        var = [_split_dot(yc[p] * yc[p], ones_bd, 2) * (1.0 / HEAD_DIM) for p in pairs]
        yn = jnp.concatenate([yc[p] * lax.rsqrt(var[p] + GN_EPS) for p in pairs], axis=1)
        o_ref[0, sl, :] = (yn * gng_ref[...] + gnb_ref[...] + bonus_ref[0, sl, :]) * g_ref[0, sl, :]
        return carry

    lax.fori_loop(0, nchunk, chunk, 0)


def _rwkv_scan(r, lw, k, v, kkn, b, g, bonus, gn_g, gn_b, tc=256, npair=8):
    B, S, W = r.shape
    tc = min(tc, S)
    npair = min(npair, W // LANES)
    C = CHUNK
    tri = jnp.tril(jnp.ones((C, C), F32)).astype(BF16)
    blk_id = jnp.arange(LANES) // HEAD_DIM
    ones_bd = (blk_id[:, None] == blk_id[None, :]).astype(BF16)
    wl = npair * LANES
    seq = pl.BlockSpec((1, tc, wl), lambda bb, p, t: (bb, t, p))
    vec = pl.BlockSpec((1, wl), lambda bb, p, t: (0, p))
    return pl.pallas_call(
        functools.partial(_rwkv_scan_kernel, nchunk=tc // C, npair=npair),
        grid=(B, W // wl, S // tc),
        in_specs=[seq] * 8 + [vec, vec,
                              pl.BlockSpec((C, C), lambda bb, p, t: (0, 0)),
                              pl.BlockSpec((LANES, LANES), lambda bb, p, t: (0, 0))],
        out_specs=seq,
        out_shape=jax.ShapeDtypeStruct((B, S, W), F32),
        scratch_shapes=[pltpu.VMEM((npair, LANES, LANES), F32)],
        compiler_params=_params(("parallel", "parallel", "arbitrary")),
        name="rwkv_scan",
    )(r, lw, k, v, kkn, b, g, bonus, gn_g, gn_b, tri, ones_bd)


def _mix_kernel(fo_ref, ro_ref, x_ref, wf_ref, wr_ref, fg_ref, g_ref, b_ref, o_ref, *, alpha):
    fo = fo_ref[...]
    fo = fo * lax.rsqrt(jnp.mean(fo * fo, axis=-1, keepdims=True) + FOX_EPS) * fg_ref[...]
    mix = jnp.dot(fo.astype(BF16), wf_ref[...], preferred_element_type=F32)
    mix = mix + jnp.dot(ro_ref[...].astype(BF16), wr_ref[...], preferred_element_type=F32)
    o_ref[...] = _layer_norm(alpha * x_ref[...] + mix, g_ref[...], b_ref[...])


def _mix(fo, ro, x, w_fox, w_rwkv, fox_g, g, b, alpha, tm=256):
    T, D = x.shape
    WF, WR = fo.shape[1], ro.shape[1]
    tm = min(tm, T)
    row = lambda i: (i, 0)
    vec = lambda i: (0, 0)
    return pl.pallas_call(
        functools.partial(_mix_kernel, alpha=alpha),
        grid=(T // tm,),
        in_specs=[pl.BlockSpec((tm, WF), row), pl.BlockSpec((tm, WR), row), pl.BlockSpec((tm, D), row),
                  pl.BlockSpec((WF, D), vec), pl.BlockSpec((WR, D), vec),
                  pl.BlockSpec((1, WF), vec), pl.BlockSpec((1, D), vec), pl.BlockSpec((1, D), vec)],
        out_specs=pl.BlockSpec((tm, D), row),
        out_shape=jax.ShapeDtypeStruct((T, D), F32),
        compiler_params=_params(("parallel",)),
        name="mix_out",
    )(fo, ro, x, w_fox, w_rwkv, fox_g, g, b)


def _xattn_kernel(x_ref, wq_ref, kv_k_ref, kv_v_ref, wo_ref, g_ref, b_ref, o_ref, *, alpha, heads):
    x = x_ref[0]
    D = x.shape[1]
    dh = D // heads
    q = jnp.dot(x.astype(BF16), wq_ref[...], preferred_element_type=F32).astype(BF16)
    outs = []
    for h in range(heads):
        cols = slice(h * dh, (h + 1) * dh)
        s = _dot_nt(q[:, cols], kv_k_ref[0, :, cols]) * (dh ** -0.5)
        s = s - jnp.max(s, axis=1, keepdims=True)
        e = jnp.exp(s)
        p = e / jnp.sum(e, axis=1, keepdims=True)
        outs.append(jnp.dot(p.astype(BF16), kv_v_ref[0, :, cols], preferred_element_type=F32).astype(BF16))
    o = jnp.concatenate(outs, axis=1)
    xa = jnp.dot(o, wo_ref[...], preferred_element_type=F32)
    o_ref[0] = _layer_norm(alpha * x + xa, g_ref[...], b_ref[...])


def _xattn(x, kv, wq, wo, g, b, alpha, tm=256):
    B, S, D = x.shape
    M = kv.shape[1]
    tm = min(tm, S)
    vec = lambda bb, i: (0, 0)
    return pl.pallas_call(
        functools.partial(_xattn_kernel, alpha=alpha, heads=XA_HEADS),
        grid=(B, S // tm),
        in_specs=[pl.BlockSpec((1, tm, D), lambda bb, i: (bb, i, 0)),
                  pl.BlockSpec((D, D), vec),
                  pl.BlockSpec((1, M, D), lambda bb, i: (bb, 0, 0)),
                  pl.BlockSpec((1, M, D), lambda bb, i: (bb, 0, 1)),
                  pl.BlockSpec((D, D), vec),
                  pl.BlockSpec((1, D), vec), pl.BlockSpec((1, D), vec)],
        out_specs=pl.BlockSpec((1, tm, D), lambda bb, i: (bb, i, 0)),
        out_shape=jax.ShapeDtypeStruct((B, S, D), F32),
        compiler_params=_params(("parallel", "arbitrary")),
        name="xattn",
    )(x, wq, kv, kv, wo, g, b)


def _pad_cols(w, n):
    return jnp.pad(w, ((0, 0), (0, n - w.shape[1])))


def _pad_rows(w, n):
    return jnp.pad(w, ((0, n - w.shape[0]), (0, 0)))


def kernel(x, mem, ln_g, ln_b, ffn1_w13, ffn1_w2, w_in, fox_f_bias, fox_out_g, rwkv_mu, rwkv_w0, rwkv_w_up, rwkv_a0, rwkv_a_up, rwkv_g_up, rwkv_k_k, rwkv_k_a, rwkv_r_k, rwkv_gn_g, rwkv_gn_b, w_mix_out, xa_wq, xa_wkv, xa_wo, ffn2_w13, ffn2_w2):
    B, S, D = x.shape
    depth = ln_g.shape[0]
    M = mem.shape[1]
    alpha = (2.0 * depth) ** 0.25
    fox_heads = fox_f_bias.shape[1]
    WF = fox_heads * HEAD_DIM
    WR = rwkv_w0.shape[1]
    decay_lora, aaa_lora, gate_lora = rwkv_w_up.shape[1], rwkv_a_up.shape[1], rwkv_g_up.shape[1]
    c1 = 3 * WF
    c2 = c1 + fox_heads
    T = B * S
    row = lambda t: t.reshape(1, -1)

    h = x.reshape(T, D)
    mem_b = mem.reshape(B * M, D).astype(BF16)
    for l in range(depth):
        h, hb = _ffn(h, ffn1_w13[l].astype(BF16), ffn1_w2[l].astype(BF16), row(ln_g[l, 0]), row(ln_b[l, 0]), alpha)

        wl = w_in[l]
        w_fox = wl[:, :c1].astype(BF16)
        w_gate = _pad_cols(wl[:, c1:c2], LANES).astype(BF16)
        o = c2 + 3 * WR
        w_rw = jnp.concatenate([wl[:, c2:o],
                                _pad_cols(wl[:, o:o + decay_lora], LANES),
                                _pad_cols(wl[:, o + decay_lora:o + decay_lora + aaa_lora], LANES),
                                _pad_cols(wl[:, o + decay_lora + aaa_lora:], 2 * LANES)], axis=1).astype(BF16)
        mu = rwkv_mu[l]
        o = 3 * WR
        mu_p = jnp.concatenate([mu[:o],
                                jnp.pad(mu[o:o + decay_lora], (0, LANES - decay_lora)),
                                jnp.pad(mu[o + decay_lora:o + decay_lora + aaa_lora], (0, LANES - aaa_lora)),
                                jnp.pad(mu[o + decay_lora + aaa_lora:], (0, 2 * LANES - gate_lora))])

        qkv = _matmul(hb, w_fox, BF16).reshape(B, S, c1)
        u = _matmul(hb, w_rw, F32).reshape(B, S, -1)
        bias = jnp.pad(fox_f_bias[l], (0, LANES - fox_heads)).reshape(1, LANES)
        fo = _fox_attention(qkv, _gate_cumsum(hb.reshape(B, S, D), w_gate, bias))

        r, lw, k2, v, kkn, bb, g, bonus = _rwkv_prep(
            u, row(mu_p), row(rwkv_w0[l]), row(rwkv_a0[l]), row(rwkv_k_k[l]), row(rwkv_k_a[l]),
            row(rwkv_r_k[l]),
            _pad_rows(rwkv_w_up[l], LANES).astype(BF16), _pad_rows(rwkv_a_up[l], LANES).astype(BF16),
            _pad_rows(rwkv_g_up[l], 2 * LANES).astype(BF16))
        ro = _rwkv_scan(r, lw, k2, v, kkn, bb, g, bonus, row(rwkv_gn_g[l]), row(rwkv_gn_b[l]))

        wm = w_mix_out[l].astype(BF16)
        h = _mix(fo.reshape(T, WF), ro.reshape(T, WR), h, wm[:WF], wm[WF:], row(fox_out_g[l]),
                 row(ln_g[l, 1]), row(ln_b[l, 1]), alpha)

        kv = _matmul(mem_b, xa_wkv[l].astype(BF16), BF16).reshape(B, M, 2 * D)
        h = _xattn(h.reshape(B, S, D), kv, xa_wq[l].astype(BF16), xa_wo[l].astype(BF16),
                   row(ln_g[l, 2]), row(ln_b[l, 2]), alpha).reshape(T, D)

        h, _ = _ffn(h, ffn2_w13[l].astype(BF16), ffn2_w2[l].astype(BF16), row(ln_g[l, 3]), row(ln_b[l, 3]), alpha)
    return h.reshape(B, S, D)
```

```python
import functools

import jax
import jax.numpy as jnp
from jax import lax
from jax.experimental import pallas as pl
from jax.experimental.pallas import tpu as pltpu

HEAD_DIM = 64
XA_HEADS = 4
LN_EPS = 1e-5
GN_EPS = 64e-5
FOX_EPS = 1e-6
NEG_INF = -1e30
LANES = 128
CHUNK = 64
VMEM_LIMIT = 56 * 1024 * 1024

F32 = jnp.float32
BF16 = jnp.bfloat16


def _params(semantics):
    return pltpu.CompilerParams(dimension_semantics=semantics, vmem_limit_bytes=VMEM_LIMIT)


def _layer_norm(z, g, b):
    mu = jnp.mean(z, axis=-1, keepdims=True)
    zc = z - mu
    var = jnp.mean(zc * zc, axis=-1, keepdims=True)
    return zc * lax.rsqrt(var + LN_EPS) * g + b


def _split_dot(a, b01, terms):
    acc = None
    rem = a
    for _ in range(terms):
        part = rem.astype(BF16)
        d = jnp.dot(part, b01, preferred_element_type=F32)
        acc = d if acc is None else acc + d
        rem = rem - part.astype(F32)
    return acc


def _split_dot_rhs(a01, b, terms):
    acc = None
    rem = b
    for _ in range(terms):
        part = rem.astype(BF16)
        d = jnp.dot(a01, part, preferred_element_type=F32)
        acc = d if acc is None else acc + d
        rem = rem - part.astype(F32)
    return acc


def _dot_nt(a, b):
    return lax.dot_general(a, b, (((1,), (1,)), ((), ())), preferred_element_type=F32)


def _dot_tn(a, b):
    return lax.dot_general(a, b, (((0,), (0,)), ((), ())), preferred_element_type=F32)


def _ffn_kernel(x_ref, wg_ref, wu_ref, w2_ref, g_ref, b_ref, o_ref, ob_ref, acc_ref, xb_ref, *, alpha):
    j = pl.program_id(1)

    @pl.when(j == 0)
    def _():
        xb_ref[...] = x_ref[...].astype(BF16)
        acc_ref[...] = jnp.zeros_like(acc_ref)

    xb = xb_ref[...]
    gate = jnp.dot(xb, wg_ref[...], preferred_element_type=F32)
    up = jnp.dot(xb, wu_ref[...], preferred_element_type=F32)
    h = (gate * jax.nn.sigmoid(gate) * up).astype(BF16)
    acc_ref[...] += jnp.dot(h, w2_ref[...], preferred_element_type=F32)

    @pl.when(j == pl.num_programs(1) - 1)
    def _():
        y = _layer_norm(alpha * x_ref[...] + 0.5 * acc_ref[...], g_ref[...], b_ref[...])
        o_ref[...] = y
        ob_ref[...] = y.astype(BF16)


def _ffn(x, w13, w2, g, b, alpha, tm=512, tf=512):
    T, D = x.shape
    F = w2.shape[0]
    tm, tf = min(tm, T), min(tf, F)
    nf = F // tf
    row = lambda i, j: (i, 0)
    vec = lambda i, j: (0, 0)
    return pl.pallas_call(
        functools.partial(_ffn_kernel, alpha=alpha),
        grid=(T // tm, nf),
        in_specs=[pl.BlockSpec((tm, D), row),
                  pl.BlockSpec((D, tf), lambda i, j: (0, j)),
                  pl.BlockSpec((D, tf), lambda i, j: (0, j + nf)),
                  pl.BlockSpec((tf, D), lambda i, j: (j, 0)),
                  pl.BlockSpec((1, D), vec),
                  pl.BlockSpec((1, D), vec)],
        out_specs=[pl.BlockSpec((tm, D), row), pl.BlockSpec((tm, D), row)],
        out_shape=[jax.ShapeDtypeStruct((T, D), F32), jax.ShapeDtypeStruct((T, D), BF16)],
        scratch_shapes=[pltpu.VMEM((tm, D), F32), pltpu.VMEM((tm, D), BF16)],
        compiler_params=_params(("parallel", "arbitrary")),
        name="ffn",
    )(x, w13, w13, w2, g, b)


def _mm_kernel(x_ref, w_ref, o_ref):
    o_ref[...] = jnp.dot(x_ref[...], w_ref[...], preferred_element_type=F32).astype(o_ref.dtype)


def _matmul(x, w, out_dtype, tm=1024, tn_max=1792):
    T, K = x.shape
    N = w.shape[1]
    tm = min(tm, T)
    tn = max(c for c in range(LANES, min(tn_max, N) + 1, LANES) if N % c == 0)
    return pl.pallas_call(
        _mm_kernel,
        grid=(T // tm, N // tn),
        in_specs=[pl.BlockSpec((tm, K), lambda i, j: (i, 0)),
                  pl.BlockSpec((K, tn), lambda i, j: (0, j))],
        out_specs=pl.BlockSpec((tm, tn), lambda i, j: (i, j)),
        out_shape=jax.ShapeDtypeStruct((T, N), out_dtype),
        compiler_params=_params(("parallel", "arbitrary")),
        name="proj",
    )(x, w)


def _gate_kernel(x_ref, w_ref, bias_ref, tri_ref, c_ref, carry_ref):
    @pl.when(pl.program_id(1) == 0)
    def _():
        carry_ref[...] = jnp.zeros_like(carry_ref)

    z = jnp.dot(x_ref[0], w_ref[...], preferred_element_type=F32) + bias_ref[...]
    logf = jnp.minimum(z, 0.0) - jnp.log(1.0 + jnp.exp(-jnp.abs(z)))
    c = _split_dot_rhs(tri_ref[...], logf, 3) + carry_ref[...]
    c_ref[0] = c
    ts = c.shape[0]
    carry_ref[...] = c[ts - 1:ts, :]


def _gate_cumsum(xb, w_gate, bias, ts=512):
    B, S, D = xb.shape
    ts = min(ts, S)
    tri = jnp.tril(jnp.ones((ts, ts), F32)).astype(BF16)
    return pl.pallas_call(
        _gate_kernel,
        grid=(B, S // ts),
        in_specs=[pl.BlockSpec((1, ts, D), lambda b, s: (b, s, 0)),
                  pl.BlockSpec((D, LANES), lambda b, s: (0, 0)),
                  pl.BlockSpec((1, LANES), lambda b, s: (0, 0)),
                  pl.BlockSpec((ts, ts), lambda b, s: (0, 0))],
        out_specs=pl.BlockSpec((1, ts, LANES), lambda b, s: (b, s, 0)),
        out_shape=jax.ShapeDtypeStruct((B, S, LANES), F32),
        scratch_shapes=[pltpu.VMEM((1, LANES), F32)],
        compiler_params=_params(("parallel", "arbitrary")),
        name="fox_gate",
    )(xb, w_gate, bias, tri)


def _fox_kernel(q_ref, k_ref, v_ref, cc_ref, o_ref, kp_ref, qb_ref, vt_ref, s_scr, p_scr, acc_scr,
                *, tq, prep_rows, scale):
    p = pl.program_id(1)
    i = pl.program_id(2)
    S = k_ref.shape[1]

    @pl.when(i == 0)
    def _():
        rows = lax.broadcasted_iota(jnp.int32, (LANES, LANES), 0)
        cols = lax.broadcasted_iota(jnp.int32, (LANES, LANES), 1)
        ident = (rows == cols).astype(BF16)
        lane = lax.broadcasted_iota(jnp.int32, (prep_rows, LANES), 1)

        def prep(n, carry):
            sl = pl.ds(pl.multiple_of(n * prep_rows, prep_rows), prep_rows)
            kk = k_ref[0, sl, :]
            vv = v_ref[0, sl, :]
            cc = cc_ref[0, sl, :]
            c1 = cc.astype(BF16)
            rem = cc - c1.astype(F32)
            c2 = rem.astype(BF16)
            c3 = (rem - c2.astype(F32)).astype(BF16)
            placed = None
            for t, part in enumerate((c1, c2, c3)):
                sel = jnp.zeros((LANES, LANES), F32)
                for e in range(2):
                    base = HEAD_DIM * (1 - e)
                    mine = rows == 2 * p + e
                    sel = sel + (mine & (cols == base + t)).astype(F32) - (mine & (cols == base + 3 + t)).astype(F32)
                d = jnp.dot(part, sel.astype(BF16), preferred_element_type=F32)
                placed = d if placed is None else placed + d
            pos = lane & (HEAD_DIM - 1)
            first3 = pos < 3
            next3 = (pos >= 3) & (pos < 6)
            k_bias = jnp.where(first3, 1.0, jnp.where(next3, placed, 0.0)).astype(BF16)
            qb_ref[sl, :] = jnp.where(first3, placed, jnp.where(next3, 1.0, 0.0)).astype(BF16)
            for e in range(2):
                in_head = (lane >= HEAD_DIM * e) & (lane < HEAD_DIM * (e + 1))
                kp_ref[e, sl, :] = jnp.where(in_head, kk, k_bias)
                v_aug = jnp.where(in_head, vv, jnp.ones_like(vv))
                vt_ref[e, :, sl] = _dot_nt(ident, v_aug).astype(BF16)
            return carry

        lax.fori_loop(0, S // prep_rows, prep, 0)

    q = q_ref[0] * scale
    lane = lax.broadcasted_iota(jnp.int32, (tq, LANES), 1)
    key_minus_qry = (lax.broadcasted_iota(jnp.int32, (tq, tq), 0) - lax.broadcasted_iota(jnp.int32, (tq, tq), 1))
    q_rows = pl.ds(pl.multiple_of(i * tq, tq), tq)
    heads = range(2)
    qp = [jnp.where((lane >= HEAD_DIM * e) & (lane < HEAD_DIM * (e + 1)), q, qb_ref[q_rows, :]) for e in heads]
    last_tile = S // tq - 1

    def tile_rows(t, clamp):
        t = jnp.minimum(t, last_tile) if clamp else t
        return pl.ds(pl.multiple_of(t * tq, tq), tq)

    def scores(t, slot, masked):
        rows = tile_rows(t, masked)
        s = [_dot_nt(kp_ref[e, rows, :], qp[e]) for e in heads]
        if masked:
            keep = key_minus_qry <= (i - t) * tq
            s = [jnp.where(keep, s[e], NEG_INF) for e in heads]
        for e in heads:
            s_scr[slot, e] = s[e]
        return [jnp.max(s[e], axis=0, keepdims=True) for e in heads]

    def probabilities(slot, tile_max, m):
        m_new = [jnp.maximum(m[e], tile_max[e]) for e in heads]
        for e in heads:
            p_scr[slot, e] = jnp.exp(s_scr[slot, e] - m_new[e]).astype(BF16)
        return m_new, [jnp.exp(m[e] - m_new[e]) for e in heads]

    def accumulate(t, slot, clamp, rescale):
        cols = tile_rows(t, clamp)
        for e in heads:
            acc_scr[e] = rescale[e] * acc_scr[e] + jnp.dot(vt_ref[e, :, cols], p_scr[slot, e],
                                                           preferred_element_type=F32)

    def advance(j, state, masked, stages=3):
        tile_max, m, rescale = state
        slot = j[1]
        accumulate(j[0], slot, masked, rescale)
        if stages >= 2:
            m, rescale = probabilities(1 - slot, tile_max, m)
        if stages >= 3:
            tile_max = scores(j[0] + 2, slot, masked)
        return tile_max, m, rescale

    acc_scr[...] = jnp.zeros_like(acc_scr)
    m, rescale = probabilities(0, scores(0, 0, True), [jnp.full((1, tq), NEG_INF, F32)] * 2)
    state = (scores(1, 1, True), m, rescale)
    n_pairs = jnp.maximum(i - 2, 0) // 2

    def body(jj, carry):
        state = tuple(list(carry[2 * n:2 * n + 2]) for n in range(3))
        state = advance((2 * jj, 0), state, False)
        state = advance((2 * jj + 1, 1), state, False)
        return tuple(state[0] + state[1] + state[2])

    carry = lax.fori_loop(0, n_pairs, body, tuple(state[0] + state[1] + state[2]))
    state = tuple(list(carry[2 * n:2 * n + 2]) for n in range(3))
    j0 = 2 * n_pairs
    state = advance((j0, 0), state, True)
    state = advance((j0 + 1, 1), state, True)
    state = advance((j0 + 2, 0), state, True, stages=2)
    advance((j0 + 3, 1), state, True, stages=1)

    outs = [acc_scr[e] / acc_scr[e, HEAD_DIM * (1 - e):HEAD_DIM * (1 - e) + 1, :] for e in heads]
    head_row = lax.broadcasted_iota(jnp.int32, (LANES, tq), 0)
    o_ref[0] = jnp.where(head_row < HEAD_DIM, outs[0], outs[1]).T


def _fox_attention(qkv, c, tq=256):
    B, S, W3 = qkv.shape
    W = W3 // 3
    npair = W // LANES
    tq = min(tq, S)
    return pl.pallas_call(
        functools.partial(_fox_kernel, tq=tq, prep_rows=min(512, S), scale=HEAD_DIM ** -0.5),
        grid=(B, npair, S // tq),
        in_specs=[pl.BlockSpec((1, tq, LANES), lambda b, p, i: (b, i, p)),
                  pl.BlockSpec((1, S, LANES), lambda b, p, i: (b, 0, npair + p)),
                  pl.BlockSpec((1, S, LANES), lambda b, p, i: (b, 0, 2 * npair + p)),
                  pl.BlockSpec((1, S, LANES), lambda b, p, i: (b, 0, 0))],
        out_specs=pl.BlockSpec((1, tq, LANES), lambda b, p, i: (b, i, p)),
        out_shape=jax.ShapeDtypeStruct((B, S, W), F32),
        scratch_shapes=[pltpu.VMEM((2, S, LANES), BF16), pltpu.VMEM((S, LANES), BF16),
                        pltpu.VMEM((2, LANES, S), BF16),
                        pltpu.VMEM((2, 2, tq, tq), F32), pltpu.VMEM((2, 2, tq, tq), BF16),
                        pltpu.VMEM((2, LANES, tq), F32)],
        compiler_params=_params(("parallel", "parallel", "arbitrary")),
        name="fox_attn",
    )(qkv, qkv, qkv, c)


def _rwkv_prep_kernel(u_ref, up_ref, mu_ref, w0_ref, a0_ref, kk_ref, ka_ref, rk_ref, wup_ref, aup_ref, gup_ref,
                      seg_ref, segt_ref,
                      r_o, lw_o, k_o, v_o, kkn_o, b_o, g_o, bonus_o, *, tm, width):
    i = pl.program_id(1)
    first_row = lax.broadcasted_iota(jnp.int32, (tm, 1), 0) == 0
    has_prev = i > 0

    def shifted(lo, hi):
        u = u_ref[0, :, lo:hi]
        prev_last = jnp.where(has_prev, up_ref[0, 7:8, lo:hi], 0.0)
        u_prev = jnp.where(first_row, prev_last, pltpu.roll(u, 1, 0))
        return u + mu_ref[:, lo:hi] * (u_prev - u)

    def head_sum(t):
        per_head = _split_dot(t, seg_ref[...], 3)
        return _split_dot(per_head, segt_ref[...], 3)

    W = width
    r = shifted(0, W)
    k = shifted(W, 2 * W)
    v = shifted(2 * W, 3 * W)
    wd = shifted(3 * W, 3 * W + LANES)
    ad = shifted(3 * W + LANES, 3 * W + 2 * LANES)
    gd = shifted(3 * W + 2 * LANES, 3 * W + 4 * LANES)

    wlin = w0_ref[...] + jnp.dot(jnp.tanh(wd).astype(BF16), wup_ref[...], preferred_element_type=F32)
    z = -wlin
    w = -(jnp.maximum(z, 0.0) + jnp.log(1.0 + jnp.exp(-jnp.abs(z)))) - 0.5
    lw_o[0] = -jnp.exp(w)
    a = jax.nn.sigmoid(a0_ref[...] + jnp.dot(ad.astype(BF16), aup_ref[...], preferred_element_type=F32))
    g_o[0] = jnp.dot(jax.nn.sigmoid(gd).astype(BF16), gup_ref[...], preferred_element_type=F32)
    kk = k * kk_ref[...]
    kkn = kk / jnp.maximum(jnp.sqrt(head_sum(kk * kk)), 1e-12)
    k2 = k * (1.0 + (a - 1.0) * ka_ref[...])
    r_o[0] = r
    k_o[0] = k2
    v_o[0] = v
    kkn_o[0] = kkn
    b_o[0] = a * kkn
    bonus_o[0] = head_sum(r * k2 * rk_ref[...]) * v


def _rwkv_prep(u, mu, w0, a0, k_k, k_a, r_k, w_up, a_up, g_up, tm=256):
    B, S, UW = u.shape
    W = w0.shape[1]
    tm = min(tm, S)
    nh = W // HEAD_DIM
    head_of = jnp.arange(W) // HEAD_DIM
    seg = (head_of[:, None] == jnp.arange(LANES)[None, :]).astype(BF16)
    segt = seg.T
    del nh
    blk = lambda b, i: (b, i, 0)
    vec = lambda b, i: (0, 0)
    out = jax.ShapeDtypeStruct((B, S, W), F32)
    return pl.pallas_call(
        functools.partial(_rwkv_prep_kernel, tm=tm, width=W),
        grid=(B, S // tm),
        in_specs=[pl.BlockSpec((1, tm, UW), blk),
                  pl.BlockSpec((1, 8, UW), lambda b, i: (b, jnp.maximum(i * (tm // 8) - 1, 0), 0)),
                  pl.BlockSpec((1, UW), vec),
                  pl.BlockSpec((1, W), vec), pl.BlockSpec((1, W), vec), pl.BlockSpec((1, W), vec),
                  pl.BlockSpec((1, W), vec), pl.BlockSpec((1, W), vec),
                  pl.BlockSpec((LANES, W), vec), pl.BlockSpec((LANES, W), vec), pl.BlockSpec((2 * LANES, W), vec),
                  pl.BlockSpec((W, LANES), vec), pl.BlockSpec((LANES, W), vec)],
        out_specs=[pl.BlockSpec((1, tm, W), blk)] * 8,
        out_shape=[out] * 8,
        compiler_params=_params(("parallel", "arbitrary")),
        name="rwkv_prep",
    )(u, u, mu, w0, a0, k_k, k_a, r_k, w_up, a_up, g_up, seg, segt)


def _rwkv_scan_kernel(r_ref, lw_ref, k_ref, v_ref, kkn_ref, b_ref, g_ref, bonus_ref, gng_ref, gnb_ref,
                      tri_ref, ones_ref, o_ref, s_ref, *, nchunk, npair):
    C = CHUNK

    @pl.when(pl.program_id(2) == 0)
    def _():
        s_ref[...] = jnp.zeros_like(s_ref)

    t_i = lax.broadcasted_iota(jnp.int32, (C, LANES), 0)
    lane = lax.broadcasted_iota(jnp.int32, (C, LANES), 1)
    s_i = lane & (HEAD_DIM - 1)
    low = lane < HEAD_DIM
    strict = s_i < t_i
    incl = s_i <= t_i
    eye = (s_i == t_i).astype(F32)
    levels = []
    k_ = 0
    while (1 << k_) < C:
        levels.append(((t_i >> k_) == ((s_i >> k_) + 1)) & ((t_i >> (k_ + 1)) == (s_i >> (k_ + 1))))
        k_ += 1
    blk_r = lax.broadcasted_iota(jnp.int32, (LANES, LANES), 0) // HEAD_DIM
    blk_c = lax.broadcasted_iota(jnp.int32, (LANES, LANES), 1) // HEAD_DIM
    same_head = blk_r == blk_c

    def block_diag(y):
        zero = jnp.zeros_like(y)
        return jnp.concatenate([jnp.where(low, y, zero), jnp.where(low, zero, y)], axis=0)

    def head_dot(x, y):
        return jnp.dot(x.astype(BF16), block_diag(y.astype(BF16)), preferred_element_type=F32)

    tri = tri_ref[...]
    ones_bd = ones_ref[...]

    def chunk(c, carry):
        sl = pl.ds(pl.multiple_of(c * C, C), C)
        pairs = range(npair)
        col = lambda t, p: t[:, p * LANES:(p + 1) * LANES]
        r = r_ref[0, sl, :]
        lw = lw_ref[0, sl, :]
        k = k_ref[0, sl, :]
        v = v_ref[0, sl, :]
        kkn = kkn_ref[0, sl, :]
        b = b_ref[0, sl, :]

        cum = _split_dot_rhs(tri, lw, 2)
        cum_last = cum[C - 1:C, :]
        g_inv = jnp.exp(-cum)
        g_tail = jnp.exp(cum_last - cum)
        a_t = (-kkn * jnp.exp(cum - lw)).astype(BF16)
        r_t = (r * jnp.exp(cum)).astype(BF16)
        k_t = (k * g_inv).astype(BF16)
        b_t = (b * g_inv).astype(BF16)
        k_tail = (k * g_tail).astype(BF16)
        b_tail = (b * g_tail).astype(BF16)
        v16 = v.astype(BF16)
        decay_last = jnp.exp(cum_last)

        s0 = [s_ref[p] for p in pairs]
        lhs = [jnp.concatenate([col(a_t, p), col(r_t, p)], axis=0) for p in pairs]
        amat = [_dot_nt(lhs[p], jnp.concatenate([block_diag(col(k_t, p)), block_diag(col(b_t, p))], axis=0))
                for p in pairs]
        s_proj = [_dot_nt(lhs[p], s0[p].astype(BF16)) for p in pairs]
        n_ab = [jnp.where(strict, amat[p][:C, LANES:], 0.0) for p in pairs]
        inv = [eye + jnp.where(levels[0], n_ab[p], 0.0) for p in pairs]
        for lvl in levels[1:]:
            x = [head_dot(inv[p], jnp.where(lvl, n_ab[p], 0.0)) for p in pairs]
            inv = [inv[p] + head_dot(x[p], inv[p]) for p in pairs]

        vbd = [block_diag(col(v16, p)) for p in pairs]
        rhs_u = [s_proj[p][:C] + jnp.dot(jnp.where(strict, amat[p][:C, :LANES], 0.0).astype(BF16), vbd[p],
                                         preferred_element_type=F32) for p in pairs]
        u = [head_dot(inv[p], rhs_u[p]) for p in pairs]
        u16 = [u[p].astype(BF16) for p in pairs]
        y = [s_proj[p][C:] + jnp.dot(jnp.where(jnp.concatenate([incl, incl], axis=1), amat[p][C:], 0.0).astype(BF16),
                                     jnp.concatenate([vbd[p], block_diag(u16[p])], axis=0),
                                     preferred_element_type=F32) for p in pairs]
        s_upd = [_dot_tn(jnp.concatenate([col(v16, p), u16[p]], axis=0),
                         jnp.concatenate([col(k_tail, p), col(b_tail, p)], axis=0)) for p in pairs]
        s_ref[...] = jnp.stack([s0[p] * col(decay_last, p) + jnp.where(same_head, s_upd[p], 0.0) for p in pairs],
                               axis=0)

        mean = [_split_dot(y[p], ones_bd, 2) * (1.0 / HEAD_DIM) for p in pairs]
        yc = [y[p] - mean[p] for p in pairs]
        var = [_split_dot(yc[p] * yc[p], ones_bd, 2) * (1.0 / HEAD_DIM) for p in pairs]
        yn = jnp.concatenate([yc[p] * lax.rsqrt(var[p] + GN_EPS) for p in pairs], axis=1)
        o_ref[0, sl, :] = (yn * gng_ref[...] + gnb_ref[...] + bonus_ref[0, sl, :]) * g_ref[0, sl, :]
        return carry

    lax.fori_loop(0, nchunk, chunk, 0)


def _rwkv_scan(r, lw, k, v, kkn, b, g, bonus, gn_g, gn_b, tc=256, npair=8):
    B, S, W = r.shape
    tc = min(tc, S)
    npair = min(npair, W // LANES)
    C = CHUNK
    tri = jnp.tril(jnp.ones((C, C), F32)).astype(BF16)
    blk_id = jnp.arange(LANES) // HEAD_DIM
    ones_bd = (blk_id[:, None] == blk_id[None, :]).astype(BF16)
    wl = npair * LANES
    seq = pl.BlockSpec((1, tc, wl), lambda bb, p, t: (bb, t, p))
    vec = pl.BlockSpec((1, wl), lambda bb, p, t: (0, p))
    return pl.pallas_call(
        functools.partial(_rwkv_scan_kernel, nchunk=tc // C, npair=npair),
        grid=(B, W // wl, S // tc),
        in_specs=[seq] * 8 + [vec, vec,
                              pl.BlockSpec((C, C), lambda bb, p, t: (0, 0)),
                              pl.BlockSpec((LANES, LANES), lambda bb, p, t: (0, 0))],
        out_specs=seq,
        out_shape=jax.ShapeDtypeStruct((B, S, W), F32),
        scratch_shapes=[pltpu.VMEM((npair, LANES, LANES), F32)],
        compiler_params=_params(("parallel", "parallel", "arbitrary")),
        name="rwkv_scan",
    )(r, lw, k, v, kkn, b, g, bonus, gn_g, gn_b, tri, ones_bd)


def _mix_kernel(fo_ref, ro_ref, x_ref, wf_ref, wr_ref, fg_ref, g_ref, b_ref, o_ref, *, alpha):
    fo = fo_ref[...]
    fo = fo * lax.rsqrt(jnp.mean(fo * fo, axis=-1, keepdims=True) + FOX_EPS) * fg_ref[...]
    mix = jnp.dot(fo.astype(BF16), wf_ref[...], preferred_element_type=F32)
    mix = mix + jnp.dot(ro_ref[...].astype(BF16), wr_ref[...], preferred_element_type=F32)
    o_ref[...] = _layer_norm(alpha * x_ref[...] + mix, g_ref[...], b_ref[...])


def _mix(fo, ro, x, w_fox, w_rwkv, fox_g, g, b, alpha, tm=256):
    T, D = x.shape
    WF, WR = fo.shape[1], ro.shape[1]
    tm = min(tm, T)
    row = lambda i: (i, 0)
    vec = lambda i: (0, 0)
    return pl.pallas_call(
        functools.partial(_mix_kernel, alpha=alpha),
        grid=(T // tm,),
        in_specs=[pl.BlockSpec((tm, WF), row), pl.BlockSpec((tm, WR), row), pl.BlockSpec((tm, D), row),
                  pl.BlockSpec((WF, D), vec), pl.BlockSpec((WR, D), vec),
                  pl.BlockSpec((1, WF), vec), pl.BlockSpec((1, D), vec), pl.BlockSpec((1, D), vec)],
        out_specs=pl.BlockSpec((tm, D), row),
        out_shape=jax.ShapeDtypeStruct((T, D), F32),
        compiler_params=_params(("parallel",)),
        name="mix_out",
    )(fo, ro, x, w_fox, w_rwkv, fox_g, g, b)


def _xattn_kernel(x_ref, wq_ref, kv_k_ref, kv_v_ref, wo_ref, g_ref, b_ref, o_ref, *, alpha, heads):
    x = x_ref[0]
    D = x.shape[1]
    dh = D // heads
    q = jnp.dot(x.astype(BF16), wq_ref[...], preferred_element_type=F32).astype(BF16)
    outs = []
    for h in range(heads):
        cols = slice(h * dh, (h + 1) * dh)
        s = _dot_nt(q[:, cols], kv_k_ref[0, :, cols]) * (dh ** -0.5)
        s = s - jnp.max(s, axis=1, keepdims=True)
        e = jnp.exp(s)
        p = e / jnp.sum(e, axis=1, keepdims=True)
        outs.append(jnp.dot(p.astype(BF16), kv_v_ref[0, :, cols], preferred_element_type=F32).astype(BF16))
    o = jnp.concatenate(outs, axis=1)
    xa = jnp.dot(o, wo_ref[...], preferred_element_type=F32)
    o_ref[0] = _layer_norm(alpha * x + xa, g_ref[...], b_ref[...])


def _xattn(x, kv, wq, wo, g, b, alpha, tm=256):
    B, S, D = x.shape
    M = kv.shape[1]
    tm = min(tm, S)
    vec = lambda bb, i: (0, 0)
    return pl.pallas_call(
        functools.partial(_xattn_kernel, alpha=alpha, heads=XA_HEADS),
        grid=(B, S // tm),
        in_specs=[pl.BlockSpec((1, tm, D), lambda bb, i: (bb, i, 0)),
                  pl.BlockSpec((D, D), vec),
                  pl.BlockSpec((1, M, D), lambda bb, i: (bb, 0, 0)),
                  pl.BlockSpec((1, M, D), lambda bb, i: (bb, 0, 1)),
                  pl.BlockSpec((D, D), vec),
                  pl.BlockSpec((1, D), vec), pl.BlockSpec((1, D), vec)],
        out_specs=pl.BlockSpec((1, tm, D), lambda bb, i: (bb, i, 0)),
        out_shape=jax.ShapeDtypeStruct((B, S, D), F32),
        compiler_params=_params(("parallel", "arbitrary")),
        name="xattn",
    )(x, wq, kv, kv, wo, g, b)


def _pad_cols(w, n):
    return jnp.pad(w, ((0, 0), (0, n - w.shape[1])))


def _pad_rows(w, n):
    return jnp.pad(w, ((0, n - w.shape[0]), (0, 0)))


def kernel(x, mem, ln_g, ln_b, ffn1_w13, ffn1_w2, w_in, fox_f_bias, fox_out_g, rwkv_mu, rwkv_w0, rwkv_w_up, rwkv_a0, rwkv_a_up, rwkv_g_up, rwkv_k_k, rwkv_k_a, rwkv_r_k, rwkv_gn_g, rwkv_gn_b, w_mix_out, xa_wq, xa_wkv, xa_wo, ffn2_w13, ffn2_w2):
    B, S, D = x.shape
    depth = ln_g.shape[0]
    M = mem.shape[1]
    alpha = (2.0 * depth) ** 0.25
    fox_heads = fox_f_bias.shape[1]
    WF = fox_heads * HEAD_DIM
    WR = rwkv_w0.shape[1]
    decay_lora, aaa_lora, gate_lora = rwkv_w_up.shape[1], rwkv_a_up.shape[1], rwkv_g_up.shape[1]
    c1 = 3 * WF
    c2 = c1 + fox_heads
    T = B * S
    row = lambda t: t.reshape(1, -1)

    h = x.reshape(T, D)
    mem_b = mem.reshape(B * M, D).astype(BF16)
    for l in range(depth):
        h, hb = _ffn(h, ffn1_w13[l].astype(BF16), ffn1_w2[l].astype(BF16), row(ln_g[l, 0]), row(ln_b[l, 0]), alpha)

        wl = w_in[l].astype(BF16)
        w_fox = wl[:, :c1]
        w_gate = _pad_cols(wl[:, c1:c2], LANES)
        o = c2 + 3 * WR
        w_rw = jnp.concatenate([wl[:, c2:o],
                                _pad_cols(wl[:, o:o + decay_lora], LANES),
                                _pad_cols(wl[:, o + decay_lora:o + decay_lora + aaa_lora], LANES),
                                _pad_cols(wl[:, o + decay_lora + aaa_lora:], 2 * LANES)], axis=1)
        mu = rwkv_mu[l]
        o = 3 * WR
        mu_p = jnp.concatenate([mu[:o],
                                jnp.pad(mu[o:o + decay_lora], (0, LANES - decay_lora)),
                                jnp.pad(mu[o + decay_lora:o + decay_lora + aaa_lora], (0, LANES - aaa_lora)),
                                jnp.pad(mu[o + decay_lora + aaa_lora:], (0, 2 * LANES - gate_lora))])

        qkv = _matmul(hb, w_fox, BF16).reshape(B, S, c1)
        u = _matmul(hb, w_rw, F32).reshape(B, S, -1)
        bias = jnp.pad(fox_f_bias[l], (0, LANES - fox_heads)).reshape(1, LANES)
        fo = _fox_attention(qkv, _gate_cumsum(hb.reshape(B, S, D), w_gate, bias))

        r, lw, k2, v, kkn, bb, g, bonus = _rwkv_prep(
            u, row(mu_p), row(rwkv_w0[l]), row(rwkv_a0[l]), row(rwkv_k_k[l]), row(rwkv_k_a[l]),
            row(rwkv_r_k[l]),
            _pad_rows(rwkv_w_up[l], LANES).astype(BF16), _pad_rows(rwkv_a_up[l], LANES).astype(BF16),
            _pad_rows(rwkv_g_up[l], 2 * LANES).astype(BF16))
        ro = _rwkv_scan(r, lw, k2, v, kkn, bb, g, bonus, row(rwkv_gn_g[l]), row(rwkv_gn_b[l]))

        wm = w_mix_out[l].astype(BF16)
        h = _mix(fo.reshape(T, WF), ro.reshape(T, WR), h, wm[:WF], wm[WF:], row(fox_out_g[l]),
                 row(ln_g[l, 1]), row(ln_b[l, 1]), alpha)

        kv = _matmul(mem_b, xa_wkv[l].astype(BF16), BF16).reshape(B, M, 2 * D)
        h = _xattn(h.reshape(B, S, D), kv, xa_wq[l].astype(BF16), xa_wo[l].astype(BF16),
                   row(ln_g[l, 2]), row(ln_b[l, 2]), alpha).reshape(T, D)

        h, _ = _ffn(h, ffn2_w13[l].astype(BF16), ffn2_w2[l].astype(BF16), row(ln_g[l, 3]), row(ln_b[l, 3]), alpha)
    return h.reshape(B, S, D)
```

```python
import functools

import jax
import jax.numpy as jnp
from jax import lax
from jax.experimental import pallas as pl
from jax.experimental.pallas import tpu as pltpu

HEAD_DIM = 64
XA_HEADS = 4
LN_EPS = 1e-5
GN_EPS = 64e-5
FOX_EPS = 1e-6
NEG_INF = -1e30
LOG2_E = 1.4426950408889634
LANES = 128
CHUNK = 64
VMEM_LIMIT = 56 * 1024 * 1024

F32 = jnp.float32
BF16 = jnp.bfloat16


def _params(semantics):
    return pltpu.CompilerParams(dimension_semantics=semantics, vmem_limit_bytes=VMEM_LIMIT)


def _layer_norm(z, g, b):
    mu = jnp.mean(z, axis=-1, keepdims=True)
    zc = z - mu
    var = jnp.mean(zc * zc, axis=-1, keepdims=True)
    return zc * lax.rsqrt(var + LN_EPS) * g + b


def _split_dot(a, b01, terms):
    acc = None
    rem = a
    for _ in range(terms):
        part = rem.astype(BF16)
        d = jnp.dot(part, b01, preferred_element_type=F32)
        acc = d if acc is None else acc + d
        rem = rem - part.astype(F32)
    return acc


def _split_dot_rhs(a01, b, terms):
    acc = None
    rem = b
    for _ in range(terms):
        part = rem.astype(BF16)
        d = jnp.dot(a01, part, preferred_element_type=F32)
        acc = d if acc is None else acc + d
        rem = rem - part.astype(F32)
    return acc


def _dot_nt(a, b):
    return lax.dot_general(a, b, (((1,), (1,)), ((), ())), preferred_element_type=F32)


def _dot_tn(a, b):
    return lax.dot_general(a, b, (((0,), (0,)), ((), ())), preferred_element_type=F32)


def _ffn_kernel(x_ref, wg_ref, wu_ref, w2_ref, g_ref, b_ref, o_ref, ob_ref, acc_ref, xb_ref, *, alpha):
    j = pl.program_id(1)

    @pl.when(j == 0)
    def _():
        xb_ref[...] = x_ref[...].astype(BF16)
        acc_ref[...] = jnp.zeros_like(acc_ref)

    xb = xb_ref[...]
    gate = jnp.dot(xb, wg_ref[...], preferred_element_type=F32)
    up = jnp.dot(xb, wu_ref[...], preferred_element_type=F32)
    h = (gate * jax.nn.sigmoid(gate) * up).astype(BF16)
    acc_ref[...] += jnp.dot(h, w2_ref[...], preferred_element_type=F32)

    @pl.when(j == pl.num_programs(1) - 1)
    def _():
        y = _layer_norm(alpha * x_ref[...] + 0.5 * acc_ref[...], g_ref[...], b_ref[...])
        o_ref[...] = y
        ob_ref[...] = y.astype(BF16)


def _ffn(x, w13, w2, g, b, alpha, tm=512, tf=512):
    T, D = x.shape
    F = w2.shape[0]
    tm, tf = min(tm, T), min(tf, F)
    assert T % tm == 0 and F % tf == 0, (T, tm, F, tf)
    nf = F // tf
    row = lambda i, j: (i, 0)
    vec = lambda i, j: (0, 0)
    return pl.pallas_call(
        functools.partial(_ffn_kernel, alpha=alpha),
        grid=(T // tm, nf),
        in_specs=[pl.BlockSpec((tm, D), row),
                  pl.BlockSpec((D, tf), lambda i, j: (0, j)),
                  pl.BlockSpec((D, tf), lambda i, j: (0, j + nf)),
                  pl.BlockSpec((tf, D), lambda i, j: (j, 0)),
                  pl.BlockSpec((1, D), vec),
                  pl.BlockSpec((1, D), vec)],
        out_specs=[pl.BlockSpec((tm, D), row), pl.BlockSpec((tm, D), row)],
        out_shape=[jax.ShapeDtypeStruct((T, D), F32), jax.ShapeDtypeStruct((T, D), BF16)],
        scratch_shapes=[pltpu.VMEM((tm, D), F32), pltpu.VMEM((tm, D), BF16)],
        compiler_params=_params(("parallel", "arbitrary")),
        name="ffn",
    )(x, w13, w13, w2, g, b)


def _mm_kernel(x_ref, w_ref, o_ref):
    o_ref[...] = jnp.dot(x_ref[...], w_ref[...], preferred_element_type=F32).astype(o_ref.dtype)


def _matmul(x, w, out_dtype, tm=1024, tn_max=1792):
    T, K = x.shape
    N = w.shape[1]
    tm = min(tm, T)
    assert T % tm == 0, (T, tm)
    tn = max(c for c in range(LANES, min(tn_max, N) + 1, LANES) if N % c == 0)
    return pl.pallas_call(
        _mm_kernel,
        grid=(T // tm, N // tn),
        in_specs=[pl.BlockSpec((tm, K), lambda i, j: (i, 0)),
                  pl.BlockSpec((K, tn), lambda i, j: (0, j))],
        out_specs=pl.BlockSpec((tm, tn), lambda i, j: (i, j)),
        out_shape=jax.ShapeDtypeStruct((T, N), out_dtype),
        compiler_params=_params(("parallel", "arbitrary")),
        name="proj",
    )(x, w)


def _gate_kernel(x_ref, w_ref, bias_ref, tri_ref, c_ref, carry_ref):
    @pl.when(pl.program_id(1) == 0)
    def _():
        carry_ref[...] = jnp.zeros_like(carry_ref)

    z = jnp.dot(x_ref[0], w_ref[...], preferred_element_type=F32) + bias_ref[...]
    logf = jnp.minimum(z, 0.0) - jnp.log(1.0 + jnp.exp(-jnp.abs(z)))
    c = _split_dot_rhs(tri_ref[...], logf, 3) + carry_ref[...]
    c_ref[0] = c
    ts = c.shape[0]
    carry_ref[...] = c[ts - 1:ts, :]


def _gate_cumsum(xb, w_gate, bias, ts=512):
    B, S, D = xb.shape
    ts = min(ts, S)
    assert S % ts == 0, (S, ts)
    tri = jnp.tril(jnp.ones((ts, ts), F32)).astype(BF16)
    return pl.pallas_call(
        _gate_kernel,
        grid=(B, S // ts),
        in_specs=[pl.BlockSpec((1, ts, D), lambda b, s: (b, s, 0)),
                  pl.BlockSpec((D, LANES), lambda b, s: (0, 0)),
                  pl.BlockSpec((1, LANES), lambda b, s: (0, 0)),
                  pl.BlockSpec((ts, ts), lambda b, s: (0, 0))],
        out_specs=pl.BlockSpec((1, ts, LANES), lambda b, s: (b, s, 0)),
        out_shape=jax.ShapeDtypeStruct((B, S, LANES), F32),
        scratch_shapes=[pltpu.VMEM((1, LANES), F32)],
        compiler_params=_params(("parallel", "arbitrary")),
        name="fox_gate",
    )(xb, w_gate, bias, tri)


def _fox_kernel(qblk_tab, tile_tab, thr_tab, q_ref, k_ref, v_ref, cc_ref, o_ref,
                kp_ref, qp_ref, vt_ref, s_scr, p_scr, acc_scr, m_scr, *, tq, n_plain, n_masked, prep_rows, scale):
    p = pl.program_id(1)
    S = k_ref.shape[1]

    def build_operands():
        rows = lax.broadcasted_iota(jnp.int32, (LANES, LANES), 0)
        cols = lax.broadcasted_iota(jnp.int32, (LANES, LANES), 1)
        ident = (rows == cols).astype(BF16)
        lane = lax.broadcasted_iota(jnp.int32, (prep_rows, LANES), 1)

        def prep(n, carry):
            sl = pl.ds(pl.multiple_of(n * prep_rows, prep_rows), prep_rows)
            kk = k_ref[0, sl, :]
            vv = v_ref[0, sl, :]
            cc = cc_ref[0, sl, :] * LOG2_E
            c1 = cc.astype(BF16)
            rem = cc - c1.astype(F32)
            c2 = rem.astype(BF16)
            c3 = (rem - c2.astype(F32)).astype(BF16)
            placed = None
            for t, part in enumerate((c1, c2, c3)):
                sel = jnp.zeros((LANES, LANES), F32)
                for e in range(2):
                    base = HEAD_DIM * (1 - e)
                    mine = rows == 2 * p + e
                    sel = sel + (mine & (cols == base + t)).astype(F32) - (mine & (cols == base + 3 + t)).astype(F32)
                d = jnp.dot(part, sel.astype(BF16), preferred_element_type=F32)
                placed = d if placed is None else placed + d
            pos = lane & (HEAD_DIM - 1)
            first3 = pos < 3
            next3 = (pos >= 3) & (pos < 6)
            k_bias = jnp.where(first3, 1.0, jnp.where(next3, placed, 0.0)).astype(BF16)
            q_bias = jnp.where(first3, placed, jnp.where(next3, 1.0, 0.0)).astype(BF16)
            qq = q_ref[0, sl, :] * (scale * LOG2_E)
            for e in range(2):
                in_head = (lane >= HEAD_DIM * e) & (lane < HEAD_DIM * (e + 1))
                kp_ref[e, sl, :] = jnp.where(in_head, kk, k_bias)
                qp_ref[e, sl, :] = jnp.where(in_head, qq, q_bias)
                v_aug = jnp.where(in_head, vv, jnp.ones_like(vv))
                vt_ref[e, :, sl] = _dot_nt(ident, v_aug).astype(BF16)
            return carry

        lax.fori_loop(0, S // prep_rows, prep, 0)

    build_operands()

    key_minus_qry = (lax.broadcasted_iota(jnp.int32, (tq, tq), 0) - lax.broadcasted_iota(jnp.int32, (tq, tq), 1))
    head_row = lax.broadcasted_iota(jnp.int32, (LANES, tq), 0)
    heads = range(2)

    def rows_of(block):
        return pl.ds(pl.multiple_of(block * tq, tq), tq)

    def scores(n, slot, masked):
        s = [_dot_nt(kp_ref[e, rows_of(tile_tab[n]), :], qp_ref[e, rows_of(qblk_tab[n]), :]) for e in heads]
        if masked:
            keep = key_minus_qry <= thr_tab[n]
            s = [jnp.where(keep, s[e], NEG_INF) for e in heads]
        for e in heads:
            s_scr[slot, e] = s[e]
        return [jnp.max(s[e], axis=0, keepdims=True) for e in heads]

    def probabilities(n, slot, tile_max):
        blk = qblk_tab[n]
        rescale = []
        for e in heads:
            m_prev = m_scr[blk, e]
            m_new = jnp.maximum(m_prev, tile_max[e])
            m_scr[blk, e] = m_new
            p_scr[slot, e] = jnp.exp2(s_scr[slot, e] - m_new).astype(BF16)
            rescale.append(jnp.exp2(m_prev - m_new))
        return rescale

    def accumulate(n, slot, rescale):
        blk = qblk_tab[n]
        for e in heads:
            acc_scr[blk, e] = rescale[e] * acc_scr[blk, e] + jnp.dot(vt_ref[e, :, rows_of(tile_tab[n])],
                                                                     p_scr[slot, e], preferred_element_type=F32)

    def run_items(first, count, masked):
        def beat(n, slot, tile_max, rescale, stages=3):
            accumulate(n, slot, rescale)
            if stages >= 2:
                rescale = probabilities(n + 1, 1 - slot, tile_max)
            if stages >= 3:
                tile_max = scores(n + 2, slot, masked)
            return tile_max, rescale

        def body(jj, carry):
            tile_max, rescale = list(carry[:2]), list(carry[2:])
            tile_max, rescale = beat(first + 2 * jj, 0, tile_max, rescale)
            tile_max, rescale = beat(first + 2 * jj + 1, 1, tile_max, rescale)
            return tuple(tile_max + rescale)

        rescale = probabilities(first, 0, scores(first, 0, masked))
        tile_max = scores(first + 1, 1, masked)
        carry = lax.fori_loop(0, (count - 2) // 2, body, tuple(tile_max + rescale))
        tile_max, rescale = beat(first + count - 2, 0, list(carry[:2]), list(carry[2:]), stages=2)
        beat(first + count - 1, 1, tile_max, rescale, stages=1)

    acc_scr[...] = jnp.zeros_like(acc_scr)
    m_scr[...] = jnp.full_like(m_scr, NEG_INF)
    if n_plain:
        run_items(0, n_plain, False)
    run_items(n_plain, n_masked, True)

    def finish(blk, carry):
        out = [acc_scr[blk, e] / acc_scr[blk, e, HEAD_DIM * (1 - e):HEAD_DIM * (1 - e) + 1, :] for e in heads]
        o_ref[0, rows_of(blk), :] = jnp.where(head_row < HEAD_DIM, out[0], out[1]).T
        return carry

    lax.fori_loop(0, S // tq, finish, 0)


def _fox_attention(qkv, c, tq=256):
    B, S, W3 = qkv.shape
    W = W3 // 3
    npair = W // LANES
    tq = min(tq, S)
    assert S % tq == 0 and W % LANES == 0, (S, tq, W)
    nq = S // tq
    below = [(i, t, 2 * tq) for i in range(nq) for t in range(i)]
    n_plain = len(below) - len(below) % 2
    masked = below[n_plain:] + [(i, i, 0) for i in range(nq)]
    masked += [(nq - 1, nq - 1, -2 * tq)] * (len(masked) % 2)
    items = below[:n_plain] + masked
    tabs = [jnp.asarray([it[n] for it in items], jnp.int32) for n in range(3)]
    seq = lambda col: pl.BlockSpec((1, S, LANES), lambda b, p, *_: (b, 0, col(p)))
    return pl.pallas_call(
        functools.partial(_fox_kernel, tq=tq, n_plain=n_plain, n_masked=len(masked),
                          prep_rows=tq * (2 if nq % 2 == 0 else 1), scale=HEAD_DIM ** -0.5),
        grid_spec=pltpu.PrefetchScalarGridSpec(
            num_scalar_prefetch=3,
            grid=(B, npair),
            in_specs=[seq(lambda p: p), seq(lambda p: npair + p), seq(lambda p: 2 * npair + p), seq(lambda p: 0)],
            out_specs=seq(lambda p: p),
            scratch_shapes=[pltpu.VMEM((2, S, LANES), BF16), pltpu.VMEM((2, S, LANES), BF16),
                            pltpu.VMEM((2, LANES, S), BF16),
                            pltpu.VMEM((2, 2, tq, tq), F32), pltpu.VMEM((2, 2, tq, tq), BF16),
                            pltpu.VMEM((nq, 2, LANES, tq), F32), pltpu.VMEM((nq, 2, 1, tq), F32)]),
        out_shape=jax.ShapeDtypeStruct((B, S, W), F32),
        compiler_params=_params(("parallel", "parallel")),
        name="fox_attn",
    )(*tabs, qkv, qkv, qkv, c)


def _rwkv_prep_kernel(u_ref, up_ref, mu_ref, w0_ref, a0_ref, kk_ref, ka_ref, rk_ref, wup_ref, aup_ref, gup_ref,
                      seg_ref, segt_ref,
                      r_o, lw_o, k_o, v_o, kkn_o, b_o, g_o, bonus_o, *, tm, width):
    i = pl.program_id(1)
    first_row = lax.broadcasted_iota(jnp.int32, (tm, 1), 0) == 0
    has_prev = i > 0

    def shifted(lo, hi):
        u = u_ref[0, :, lo:hi]
        prev_last = jnp.where(has_prev, up_ref[0, 7:8, lo:hi], 0.0)
        u_prev = jnp.where(first_row, prev_last, pltpu.roll(u, 1, 0))
        return u + mu_ref[:, lo:hi] * (u_prev - u)

    def head_sum(t):
        per_head = _split_dot(t, seg_ref[...], 3)
        return _split_dot(per_head, segt_ref[...], 3)

    W = width
    r = shifted(0, W)
    k = shifted(W, 2 * W)
    v = shifted(2 * W, 3 * W)
    wd = shifted(3 * W, 3 * W + LANES)
    ad = shifted(3 * W + LANES, 3 * W + 2 * LANES)
    gd = shifted(3 * W + 2 * LANES, 3 * W + 4 * LANES)

    wlin = w0_ref[...] + jnp.dot(jnp.tanh(wd).astype(BF16), wup_ref[...], preferred_element_type=F32)
    z = -wlin
    w = -(jnp.maximum(z, 0.0) + jnp.log(1.0 + jnp.exp(-jnp.abs(z)))) - 0.5
    lw_o[0] = -jnp.exp(w)
    a = jax.nn.sigmoid(a0_ref[...] + jnp.dot(ad.astype(BF16), aup_ref[...], preferred_element_type=F32))
    g_o[0] = jnp.dot(jax.nn.sigmoid(gd).astype(BF16), gup_ref[...], preferred_element_type=F32)
    kk = k * kk_ref[...]
    kkn = kk / jnp.maximum(jnp.sqrt(head_sum(kk * kk)), 1e-12)
    k2 = k * (1.0 + (a - 1.0) * ka_ref[...])
    r_o[0] = r
    k_o[0] = k2
    v_o[0] = v
    kkn_o[0] = kkn
    b_o[0] = a * kkn
    bonus_o[0] = head_sum(r * k2 * rk_ref[...]) * v


def _rwkv_prep(u, mu, w0, a0, k_k, k_a, r_k, w_up, a_up, g_up, tm=256):
    B, S, UW = u.shape
    W = w0.shape[1]
    tm = min(tm, S)
    nh = W // HEAD_DIM
    head_of = jnp.arange(W) // HEAD_DIM
    seg = (head_of[:, None] == jnp.arange(LANES)[None, :]).astype(BF16)
    segt = seg.T
    del nh
    blk = lambda b, i: (b, i, 0)
    vec = lambda b, i: (0, 0)
    out = jax.ShapeDtypeStruct((B, S, W), F32)
    return pl.pallas_call(
        functools.partial(_rwkv_prep_kernel, tm=tm, width=W),
        grid=(B, S // tm),
        in_specs=[pl.BlockSpec((1, tm, UW), blk),
                  pl.BlockSpec((1, 8, UW), lambda b, i: (b, jnp.maximum(i * (tm // 8) - 1, 0), 0)),
                  pl.BlockSpec((1, UW), vec),
                  pl.BlockSpec((1, W), vec), pl.BlockSpec((1, W), vec), pl.BlockSpec((1, W), vec),
                  pl.BlockSpec((1, W), vec), pl.BlockSpec((1, W), vec),
                  pl.BlockSpec((LANES, W), vec), pl.BlockSpec((LANES, W), vec), pl.BlockSpec((2 * LANES, W), vec),
                  pl.BlockSpec((W, LANES), vec), pl.BlockSpec((LANES, W), vec)],
        out_specs=[pl.BlockSpec((1, tm, W), blk)] * 8,
        out_shape=[out] * 8,
        compiler_params=_params(("parallel", "arbitrary")),
        name="rwkv_prep",
    )(u, u, mu, w0, a0, k_k, k_a, r_k, w_up, a_up, g_up, seg, segt)


def _rwkv_scan_kernel(r_ref, lw_ref, k_ref, v_ref, kkn_ref, b_ref, g_ref, bonus_ref, gng_ref, gnb_ref,
                      tri_ref, ones_ref, o_ref, s_ref, *, nchunk, npair):
    C = CHUNK

    @pl.when(pl.program_id(2) == 0)
    def _():
        s_ref[...] = jnp.zeros_like(s_ref)

    t_i = lax.broadcasted_iota(jnp.int32, (C, LANES), 0)
    lane = lax.broadcasted_iota(jnp.int32, (C, LANES), 1)
    s_i = lane & (HEAD_DIM - 1)
    low = lane < HEAD_DIM
    strict = s_i < t_i
    incl = s_i <= t_i
    eye = (s_i == t_i).astype(F32)
    levels = []
    k_ = 0
    while (1 << k_) < C:
        levels.append(((t_i >> k_) == ((s_i >> k_) + 1)) & ((t_i >> (k_ + 1)) == (s_i >> (k_ + 1))))
        k_ += 1
    blk_r = lax.broadcasted_iota(jnp.int32, (LANES, LANES), 0) // HEAD_DIM
    blk_c = lax.broadcasted_iota(jnp.int32, (LANES, LANES), 1) // HEAD_DIM
    same_head = blk_r == blk_c

    def block_diag(y):
        zero = jnp.zeros_like(y)
        return jnp.concatenate([jnp.where(low, y, zero), jnp.where(low, zero, y)], axis=0)

    def head_dot(x, y):
        return jnp.dot(x.astype(BF16), block_diag(y.astype(BF16)), preferred_element_type=F32)

    tri = tri_ref[...]
    ones_bd = ones_ref[...]

    def chunk(c, carry):
        sl = pl.ds(pl.multiple_of(c * C, C), C)
        pairs = range(npair)
        col = lambda t, p: t[:, p * LANES:(p + 1) * LANES]
        r = r_ref[0, sl, :]
        lw = lw_ref[0, sl, :]
        k = k_ref[0, sl, :]
        v = v_ref[0, sl, :]
        kkn = kkn_ref[0, sl, :]
        b = b_ref[0, sl, :]

        cum = _split_dot_rhs(tri, lw, 2)
        cum_last = cum[C - 1:C, :]
        g_inv = jnp.exp(-cum)
        g_tail = jnp.exp(cum_last - cum)
        a_t = (-kkn * jnp.exp(cum - lw)).astype(BF16)
        r_t = (r * jnp.exp(cum)).astype(BF16)
        k_t = (k * g_inv).astype(BF16)
        b_t = (b * g_inv).astype(BF16)
        k_tail = (k * g_tail).astype(BF16)
        b_tail = (b * g_tail).astype(BF16)
        v16 = v.astype(BF16)
        decay_last = jnp.exp(cum_last)

        s0 = [s_ref[p] for p in pairs]
        lhs = [jnp.concatenate([col(a_t, p), col(r_t, p)], axis=0) for p in pairs]
        amat = [_dot_nt(lhs[p], jnp.concatenate([block_diag(col(k_t, p)), block_diag(col(b_t, p))], axis=0))
                for p in pairs]
        s_proj = [_dot_nt(lhs[p], s0[p].astype(BF16)) for p in pairs]
        n_ab = [jnp.where(strict, amat[p][:C, LANES:], 0.0) for p in pairs]
        inv = [eye + jnp.where(levels[0], n_ab[p], 0.0) for p in pairs]
        for lvl in levels[1:]:
            x = [head_dot(inv[p], jnp.where(lvl, n_ab[p], 0.0)) for p in pairs]
            inv = [inv[p] + head_dot(x[p], inv[p]) for p in pairs]

        vbd = [block_diag(col(v16, p)) for p in pairs]
        rhs_u = [s_proj[p][:C] + jnp.dot(jnp.where(strict, amat[p][:C, :LANES], 0.0).astype(BF16), vbd[p],
                                         preferred_element_type=F32) for p in pairs]
        u = [head_dot(inv[p], rhs_u[p]) for p in pairs]
        u16 = [u[p].astype(BF16) for p in pairs]
        y = [s_proj[p][C:] + jnp.dot(jnp.where(jnp.concatenate([incl, incl], axis=1), amat[p][C:], 0.0).astype(BF16),
                                     jnp.concatenate([vbd[p], block_diag(u16[p])], axis=0),
                                     preferred_element_type=F32) for p in pairs]
        s_upd = [_dot_tn(jnp.concatenate([col(v16, p), u16[p]], axis=0),
                         jnp.concatenate([col(k_tail, p), col(b_tail, p)], axis=0)) for p in pairs]
        s_ref[...] = jnp.stack([s0[p] * col(decay_last, p) + jnp.where(same_head, s_upd[p], 0.0) for p in pairs],
                               axis=0)

        mean = [_split_dot(y[p], ones_bd, 2) * (1.0 / HEAD_DIM) for p in pairs]
        yc = [y[p] - mean[p] for p in pairs]
        var = [_split_dot(yc[p] * yc[p], ones_bd, 2) * (1.0 / HEAD_DIM) for p in pairs]
        yn = jnp.concatenate([yc[p] * lax.rsqrt(var[p] + GN_EPS) for p in pairs], axis=1)
        o_ref[0, sl, :] = (yn * gng_ref[...] + gnb_ref[...] + bonus_ref[0, sl, :]) * g_ref[0, sl, :]
        return carry

    lax.fori_loop(0, nchunk, chunk, 0)


def _rwkv_scan(r, lw, k, v, kkn, b, g, bonus, gn_g, gn_b, tc=256, npair=8):
    B, S, W = r.shape
    tc = min(tc, S)
    npair = min(npair, W // LANES)
    assert S % tc == 0 and tc % CHUNK == 0 and W % (npair * LANES) == 0, (S, tc, W, npair)
    C = CHUNK
    tri = jnp.tril(jnp.ones((C, C), F32)).astype(BF16)
    blk_id = jnp.arange(LANES) // HEAD_DIM
    ones_bd = (blk_id[:, None] == blk_id[None, :]).astype(BF16)
    wl = npair * LANES
    seq = pl.BlockSpec((1, tc, wl), lambda bb, p, t: (bb, t, p))
    vec = pl.BlockSpec((1, wl), lambda bb, p, t: (0, p))
    return pl.pallas_call(
        functools.partial(_rwkv_scan_kernel, nchunk=tc // C, npair=npair),
        grid=(B, W // wl, S // tc),
        in_specs=[seq] * 8 + [vec, vec,
                              pl.BlockSpec((C, C), lambda bb, p, t: (0, 0)),
                              pl.BlockSpec((LANES, LANES), lambda bb, p, t: (0, 0))],
        out_specs=seq,
        out_shape=jax.ShapeDtypeStruct((B, S, W), F32),
        scratch_shapes=[pltpu.VMEM((npair, LANES, LANES), F32)],
        compiler_params=_params(("parallel", "parallel", "arbitrary")),
        name="rwkv_scan",
    )(r, lw, k, v, kkn, b, g, bonus, gn_g, gn_b, tri, ones_bd)


def _mix_kernel(fo_ref, ro_ref, x_ref, wf_ref, wr_ref, fg_ref, g_ref, b_ref, o_ref, *, alpha):
    fo = fo_ref[...]
    fo = fo * lax.rsqrt(jnp.mean(fo * fo, axis=-1, keepdims=True) + FOX_EPS) * fg_ref[...]
    mix = jnp.dot(fo.astype(BF16), wf_ref[...], preferred_element_type=F32)
    mix = mix + jnp.dot(ro_ref[...].astype(BF16), wr_ref[...], preferred_element_type=F32)
    o_ref[...] = _layer_norm(alpha * x_ref[...] + mix, g_ref[...], b_ref[...])


def _mix(fo, ro, x, w_fox, w_rwkv, fox_g, g, b, alpha, tm=256):
    T, D = x.shape
    WF, WR = fo.shape[1], ro.shape[1]
    tm = min(tm, T)
    row = lambda i: (i, 0)
    vec = lambda i: (0, 0)
    return pl.pallas_call(
        functools.partial(_mix_kernel, alpha=alpha),
        grid=(T // tm,),
        in_specs=[pl.BlockSpec((tm, WF), row), pl.BlockSpec((tm, WR), row), pl.BlockSpec((tm, D), row),
                  pl.BlockSpec((WF, D), vec), pl.BlockSpec((WR, D), vec),
                  pl.BlockSpec((1, WF), vec), pl.BlockSpec((1, D), vec), pl.BlockSpec((1, D), vec)],
        out_specs=pl.BlockSpec((tm, D), row),
        out_shape=jax.ShapeDtypeStruct((T, D), F32),
        compiler_params=_params(("parallel",)),
        name="mix_out",
    )(fo, ro, x, w_fox, w_rwkv, fox_g, g, b)


def _xattn_kernel(x_ref, wq_ref, kv_k_ref, kv_v_ref, wo_ref, g_ref, b_ref, o_ref, *, alpha, heads):
    x = x_ref[0]
    D = x.shape[1]
    dh = D // heads
    q = jnp.dot(x.astype(BF16), wq_ref[...], preferred_element_type=F32).astype(BF16)
    outs = []
    for h in range(heads):
        cols = slice(h * dh, (h + 1) * dh)
        s = _dot_nt(q[:, cols], kv_k_ref[0, :, cols]) * (dh ** -0.5)
        s = s - jnp.max(s, axis=1, keepdims=True)
        e = jnp.exp(s)
        p = e / jnp.sum(e, axis=1, keepdims=True)
        outs.append(jnp.dot(p.astype(BF16), kv_v_ref[0, :, cols], preferred_element_type=F32).astype(BF16))
    o = jnp.concatenate(outs, axis=1)
    xa = jnp.dot(o, wo_ref[...], preferred_element_type=F32)
    o_ref[0] = _layer_norm(alpha * x + xa, g_ref[...], b_ref[...])


def _xattn(x, kv, wq, wo, g, b, alpha, tm=256):
    B, S, D = x.shape
    M = kv.shape[1]
    tm = min(tm, S)
    vec = lambda bb, i: (0, 0)
    return pl.pallas_call(
        functools.partial(_xattn_kernel, alpha=alpha, heads=XA_HEADS),
        grid=(B, S // tm),
        in_specs=[pl.BlockSpec((1, tm, D), lambda bb, i: (bb, i, 0)),
                  pl.BlockSpec((D, D), vec),
                  pl.BlockSpec((1, M, D), lambda bb, i: (bb, 0, 0)),
                  pl.BlockSpec((1, M, D), lambda bb, i: (bb, 0, 1)),
                  pl.BlockSpec((D, D), vec),
                  pl.BlockSpec((1, D), vec), pl.BlockSpec((1, D), vec)],
        out_specs=pl.BlockSpec((1, tm, D), lambda bb, i: (bb, i, 0)),
        out_shape=jax.ShapeDtypeStruct((B, S, D), F32),
        compiler_params=_params(("parallel", "arbitrary")),
        name="xattn",
    )(x, wq, kv, kv, wo, g, b)


def _pad_cols(w, n):
    return jnp.pad(w, ((0, 0), (0, n - w.shape[1])))


def _pad_rows(w, n):
    return jnp.pad(w, ((0, n - w.shape[0]), (0, 0)))


def kernel(x, mem, ln_g, ln_b, ffn1_w13, ffn1_w2, w_in, fox_f_bias, fox_out_g, rwkv_mu, rwkv_w0, rwkv_w_up, rwkv_a0, rwkv_a_up, rwkv_g_up, rwkv_k_k, rwkv_k_a, rwkv_r_k, rwkv_gn_g, rwkv_gn_b, w_mix_out, xa_wq, xa_wkv, xa_wo, ffn2_w13, ffn2_w2):
    B, S, D = x.shape
    depth = ln_g.shape[0]
    M = mem.shape[1]
    alpha = (2.0 * depth) ** 0.25
    fox_heads = fox_f_bias.shape[1]
    WF = fox_heads * HEAD_DIM
    WR = rwkv_w0.shape[1]
    decay_lora, aaa_lora, gate_lora = rwkv_w_up.shape[1], rwkv_a_up.shape[1], rwkv_g_up.shape[1]
    c1 = 3 * WF
    c2 = c1 + fox_heads
    T = B * S
    row = lambda t: t.reshape(1, -1)

    h = x.reshape(T, D)
    mem_b = mem.reshape(B * M, D).astype(BF16)
    for l in range(depth):
        h, hb = _ffn(h, ffn1_w13[l].astype(BF16), ffn1_w2[l].astype(BF16), row(ln_g[l, 0]), row(ln_b[l, 0]), alpha)

        wl = w_in[l].astype(BF16)
        w_fox = wl[:, :c1]
        w_gate = _pad_cols(wl[:, c1:c2], LANES)
        o = c2 + 3 * WR
        w_rw = jnp.concatenate([wl[:, c2:o],
                                _pad_cols(wl[:, o:o + decay_lora], LANES),
                                _pad_cols(wl[:, o + decay_lora:o + decay_lora + aaa_lora], LANES),
                                _pad_cols(wl[:, o + decay_lora + aaa_lora:], 2 * LANES)], axis=1)
        mu = rwkv_mu[l]
        o = 3 * WR
        mu_p = jnp.concatenate([mu[:o],
                                jnp.pad(mu[o:o + decay_lora], (0, LANES - decay_lora)),
                                jnp.pad(mu[o + decay_lora:o + decay_lora + aaa_lora], (0, LANES - aaa_lora)),
                                jnp.pad(mu[o + decay_lora + aaa_lora:], (0, 2 * LANES - gate_lora))])

        qkv = _matmul(hb, w_fox, BF16).reshape(B, S, c1)
        u = _matmul(hb, w_rw, F32).reshape(B, S, -1)
        bias = jnp.pad(fox_f_bias[l], (0, LANES - fox_heads)).reshape(1, LANES)
        fo = _fox_attention(qkv, _gate_cumsum(hb.reshape(B, S, D), w_gate, bias))

        r, lw, k2, v, kkn, bb, g, bonus = _rwkv_prep(
            u, row(mu_p), row(rwkv_w0[l]), row(rwkv_a0[l]), row(rwkv_k_k[l]), row(rwkv_k_a[l]),
            row(rwkv_r_k[l]),
            _pad_rows(rwkv_w_up[l], LANES).astype(BF16), _pad_rows(rwkv_a_up[l], LANES).astype(BF16),
            _pad_rows(rwkv_g_up[l], 2 * LANES).astype(BF16))
        ro = _rwkv_scan(r, lw, k2, v, kkn, bb, g, bonus, row(rwkv_gn_g[l]), row(rwkv_gn_b[l]))

        wm = w_mix_out[l].astype(BF16)
        h = _mix(fo.reshape(T, WF), ro.reshape(T, WR), h, wm[:WF], wm[WF:], row(fox_out_g[l]),
                 row(ln_g[l, 1]), row(ln_b[l, 1]), alpha)

        kv = _matmul(mem_b, xa_wkv[l].astype(BF16), BF16).reshape(B, M, 2 * D)
        h = _xattn(h.reshape(B, S, D), kv, xa_wq[l].astype(BF16), xa_wo[l].astype(BF16),
                   row(ln_g[l, 2]), row(ln_b[l, 2]), alpha).reshape(T, D)

        h, _ = _ffn(h, ffn2_w13[l].astype(BF16), ffn2_w2[l].astype(BF16), row(ln_g[l, 3]), row(ln_b[l, 3]), alpha)
    return h.reshape(B, S, D)
```

```python
import functools

import jax
import jax.numpy as jnp
from jax import lax
from jax.experimental import pallas as pl
from jax.experimental.pallas import tpu as pltpu

HEAD_DIM = 64
XA_HEADS = 4
LN_EPS = 1e-5
GN_EPS = 64e-5
FOX_EPS = 1e-6
NEG_INF = -1e30
LOG2_E = 1.4426950408889634
LANES = 128
CHUNK = 64
FOX_BEATS_PER_TRIP = 4
VMEM_LIMIT = 56 * 1024 * 1024

F32 = jnp.float32
BF16 = jnp.bfloat16


def _params(semantics):
    return pltpu.CompilerParams(dimension_semantics=semantics, vmem_limit_bytes=VMEM_LIMIT)


def _layer_norm(z, g, b):
    mu = jnp.mean(z, axis=-1, keepdims=True)
    zc = z - mu
    var = jnp.mean(zc * zc, axis=-1, keepdims=True)
    return zc * lax.rsqrt(var + LN_EPS) * g + b


def _split_dot(a, b01, terms):
    acc = None
    rem = a
    for _ in range(terms):
        part = rem.astype(BF16)
        d = jnp.dot(part, b01, preferred_element_type=F32)
        acc = d if acc is None else acc + d
        rem = rem - part.astype(F32)
    return acc


def _split_dot_rhs(a01, b, terms):
    acc = None
    rem = b
    for _ in range(terms):
        part = rem.astype(BF16)
        d = jnp.dot(a01, part, preferred_element_type=F32)
        acc = d if acc is None else acc + d
        rem = rem - part.astype(F32)
    return acc


def _dot_nt(a, b):
    return lax.dot_general(a, b, (((1,), (1,)), ((), ())), preferred_element_type=F32)


def _dot_tn(a, b):
    return lax.dot_general(a, b, (((0,), (0,)), ((), ())), preferred_element_type=F32)


def _ffn_kernel(x_ref, wg_ref, wu_ref, w2_ref, g_ref, b_ref, o_ref, ob_ref, acc_ref, xb_ref, *, alpha):
    j = pl.program_id(1)

    @pl.when(j == 0)
    def _():
        xb_ref[...] = x_ref[...].astype(BF16)
        acc_ref[...] = jnp.zeros_like(acc_ref)

    xb = xb_ref[...]
    gate = jnp.dot(xb, wg_ref[...], preferred_element_type=F32)
    up = jnp.dot(xb, wu_ref[...], preferred_element_type=F32)
    h = (gate * jax.nn.sigmoid(gate) * up).astype(BF16)
    acc_ref[...] += jnp.dot(h, w2_ref[...], preferred_element_type=F32)

    @pl.when(j == pl.num_programs(1) - 1)
    def _():
        y = _layer_norm(alpha * x_ref[...] + 0.5 * acc_ref[...], g_ref[...], b_ref[...])
        o_ref[...] = y
        ob_ref[...] = y.astype(BF16)


def _ffn(x, w13, w2, layer, g, b, alpha, tm=512, tf=512):
    T, D = x.shape
    F = w2.shape[1]
    tm, tf = min(tm, T), min(tf, F)
    assert T % tm == 0 and F % tf == 0, (T, tm, F, tf)
    nf = F // tf
    row = lambda i, j: (i, 0)
    vec = lambda i, j: (0, 0)
    return pl.pallas_call(
        functools.partial(_ffn_kernel, alpha=alpha),
        grid=(T // tm, nf),
        in_specs=[pl.BlockSpec((tm, D), row),
                  pl.BlockSpec((None, D, tf), lambda i, j: (layer, 0, j)),
                  pl.BlockSpec((None, D, tf), lambda i, j: (layer, 0, j + nf)),
                  pl.BlockSpec((None, tf, D), lambda i, j: (layer, j, 0)),
                  pl.BlockSpec((1, D), vec),
                  pl.BlockSpec((1, D), vec)],
        out_specs=[pl.BlockSpec((tm, D), row), pl.BlockSpec((tm, D), row)],
        out_shape=[jax.ShapeDtypeStruct((T, D), F32), jax.ShapeDtypeStruct((T, D), BF16)],
        scratch_shapes=[pltpu.VMEM((tm, D), F32), pltpu.VMEM((tm, D), BF16)],
        compiler_params=_params(("parallel", "arbitrary")),
        name="ffn",
    )(x, w13, w13, w2, g, b)


def _mm_kernel(x_ref, w_ref, o_ref):
    o_ref[...] = jnp.dot(x_ref[...], w_ref[...], preferred_element_type=F32).astype(o_ref.dtype)


def _matmul(x, w, out_dtype, layer=None, tm=1024, tn_max=1792):
    T, K = x.shape
    N = w.shape[-1]
    tm = min(tm, T)
    assert T % tm == 0, (T, tm)
    tn = max(c for c in range(LANES, min(tn_max, N) + 1, LANES) if N % c == 0)
    return pl.pallas_call(
        _mm_kernel,
        grid=(T // tm, N // tn),
        in_specs=[pl.BlockSpec((tm, K), lambda i, j: (i, 0)),
                  pl.BlockSpec((K, tn), lambda i, j: (0, j)) if layer is None else
                  pl.BlockSpec((None, K, tn), lambda i, j: (layer, 0, j))],
        out_specs=pl.BlockSpec((tm, tn), lambda i, j: (i, j)),
        out_shape=jax.ShapeDtypeStruct((T, N), out_dtype),
        compiler_params=_params(("parallel", "arbitrary")),
        name="proj",
    )(x, w)


def _gate_kernel(x_ref, w_ref, bias_ref, tri_ref, c_ref, carry_ref):
    @pl.when(pl.program_id(1) == 0)
    def _():
        carry_ref[...] = jnp.zeros_like(carry_ref)

    z = jnp.dot(x_ref[0], w_ref[...], preferred_element_type=F32) + bias_ref[...]
    logf = jnp.minimum(z, 0.0) - jnp.log(1.0 + jnp.exp(-jnp.abs(z)))
    c = _split_dot_rhs(tri_ref[...], logf, 3) + carry_ref[...]
    c_ref[0] = c
    ts = c.shape[0]
    carry_ref[...] = c[ts - 1:ts, :]


def _gate_cumsum(xb, w_gate, bias, ts=512):
    B, S, D = xb.shape
    ts = min(ts, S)
    assert S % ts == 0, (S, ts)
    tri = jnp.tril(jnp.ones((ts, ts), F32)).astype(BF16)
    return pl.pallas_call(
        _gate_kernel,
        grid=(B, S // ts),
        in_specs=[pl.BlockSpec((1, ts, D), lambda b, s: (b, s, 0)),
                  pl.BlockSpec((D, LANES), lambda b, s: (0, 0)),
                  pl.BlockSpec((1, LANES), lambda b, s: (0, 0)),
                  pl.BlockSpec((ts, ts), lambda b, s: (0, 0))],
        out_specs=pl.BlockSpec((1, ts, LANES), lambda b, s: (b, s, 0)),
        out_shape=jax.ShapeDtypeStruct((B, S, LANES), F32),
        scratch_shapes=[pltpu.VMEM((1, LANES), F32)],
        compiler_params=_params(("parallel", "arbitrary")),
        name="fox_gate",
    )(xb, w_gate, bias, tri)


def _fox_kernel(qblk_tab, tile_tab, thr_tab, q_ref, k_ref, v_ref, cc_ref, o_ref,
                kp_ref, qp_ref, vt_ref, s_scr, p_scr, acc_scr, m_scr, *, tq, n_plain, n_masked, prep_rows, scale):
    p = pl.program_id(1)
    S = k_ref.shape[1]

    def build_operands():
        rows = lax.broadcasted_iota(jnp.int32, (LANES, LANES), 0)
        cols = lax.broadcasted_iota(jnp.int32, (LANES, LANES), 1)
        ident = (rows == cols).astype(BF16)
        lane = lax.broadcasted_iota(jnp.int32, (prep_rows, LANES), 1)

        def prep(n, carry):
            sl = pl.ds(pl.multiple_of(n * prep_rows, prep_rows), prep_rows)
            kk = k_ref[0, sl, :]
            vv = v_ref[0, sl, :]
            cc = cc_ref[0, sl, :] * LOG2_E
            c1 = cc.astype(BF16)
            rem = cc - c1.astype(F32)
            c2 = rem.astype(BF16)
            c3 = (rem - c2.astype(F32)).astype(BF16)
            placed = None
            for t, part in enumerate((c1, c2, c3)):
                sel = jnp.zeros((LANES, LANES), F32)
                for e in range(2):
                    base = HEAD_DIM * (1 - e)
                    mine = rows == 2 * p + e
                    sel = sel + (mine & (cols == base + t)).astype(F32) - (mine & (cols == base + 3 + t)).astype(F32)
                d = jnp.dot(part, sel.astype(BF16), preferred_element_type=F32)
                placed = d if placed is None else placed + d
            pos = lane & (HEAD_DIM - 1)
            first3 = pos < 3
            next3 = (pos >= 3) & (pos < 6)
            k_bias = jnp.where(first3, 1.0, jnp.where(next3, placed, 0.0)).astype(BF16)
            q_bias = jnp.where(first3, placed, jnp.where(next3, 1.0, 0.0)).astype(BF16)
            qq = q_ref[0, sl, :] * (scale * LOG2_E)
            for e in range(2):
                in_head = (lane >= HEAD_DIM * e) & (lane < HEAD_DIM * (e + 1))
                kp_ref[e, sl, :] = jnp.where(in_head, kk, k_bias)
                qp_ref[e, sl, :] = jnp.where(in_head, qq, q_bias)
                v_aug = jnp.where(in_head, vv, jnp.ones_like(vv))
                vt_ref[e, :, sl] = _dot_nt(ident, v_aug).astype(BF16)
            return carry

        lax.fori_loop(0, S // prep_rows, prep, 0)

    build_operands()

    key_minus_qry = (lax.broadcasted_iota(jnp.int32, (tq, tq), 0) - lax.broadcasted_iota(jnp.int32, (tq, tq), 1))
    head_row = lax.broadcasted_iota(jnp.int32, (LANES, tq), 0)
    heads = range(2)

    def rows_of(block):
        return pl.ds(pl.multiple_of(block * tq, tq), tq)

    def scores(n, slot, masked):
        s = [_dot_nt(kp_ref[e, rows_of(tile_tab[n]), :], qp_ref[e, rows_of(qblk_tab[n]), :]) for e in heads]
        if masked:
            keep = key_minus_qry <= thr_tab[n]
            s = [jnp.where(keep, s[e], NEG_INF) for e in heads]
        for e in heads:
            s_scr[slot, e] = s[e]
        return [jnp.max(s[e], axis=0, keepdims=True) for e in heads]

    def probabilities(n, slot, tile_max):
        blk = qblk_tab[n]
        rescale = []
        for e in heads:
            m_prev = m_scr[blk, e]
            m_new = jnp.maximum(m_prev, tile_max[e])
            m_scr[blk, e] = m_new
            p_scr[slot, e] = jnp.exp2(s_scr[slot, e] - m_new).astype(BF16)
            rescale.append(jnp.exp2(m_prev - m_new))
        return rescale

    def accumulate(n, slot, rescale):
        blk = qblk_tab[n]
        for e in heads:
            acc_scr[blk, e] = rescale[e] * acc_scr[blk, e] + jnp.dot(vt_ref[e, :, rows_of(tile_tab[n])],
                                                                     p_scr[slot, e], preferred_element_type=F32)

    def run_items(first, count, masked):
        def beat(n, slot, tile_max, rescale, stages=3):
            accumulate(n, slot, rescale)
            if stages >= 2:
                rescale = probabilities(n + 1, 1 - slot, tile_max)
            if stages >= 3:
                tile_max = scores(n + 2, slot, masked)
            return tile_max, rescale

        def body(jj, carry):
            tile_max, rescale = list(carry[:2]), list(carry[2:])
            for k in range(FOX_BEATS_PER_TRIP):
                tile_max, rescale = beat(first + FOX_BEATS_PER_TRIP * jj + k, k % 2, tile_max, rescale)
            return tuple(tile_max + rescale)

        rescale = probabilities(first, 0, scores(first, 0, masked))
        tile_max = scores(first + 1, 1, masked)
        trips, rest = divmod(count - 2, FOX_BEATS_PER_TRIP)
        carry = lax.fori_loop(0, trips, body, tuple(tile_max + rescale))
        tile_max, rescale = list(carry[:2]), list(carry[2:])
        done = trips * FOX_BEATS_PER_TRIP
        for k in range(rest):
            tile_max, rescale = beat(first + done + k, k % 2, tile_max, rescale)
        tile_max, rescale = beat(first + count - 2, 0, tile_max, rescale, stages=2)
        beat(first + count - 1, 1, tile_max, rescale, stages=1)

    acc_scr[...] = jnp.zeros_like(acc_scr)
    m_scr[...] = jnp.full_like(m_scr, NEG_INF)
    if n_plain:
        run_items(0, n_plain, False)
    run_items(n_plain, n_masked, True)

    def finish(blk, carry):
        out = [acc_scr[blk, e] / acc_scr[blk, e, HEAD_DIM * (1 - e):HEAD_DIM * (1 - e) + 1, :] for e in heads]
        o_ref[0, rows_of(blk), :] = jnp.where(head_row < HEAD_DIM, out[0], out[1]).T
        return carry

    lax.fori_loop(0, S // tq, finish, 0)


def _fox_attention(qkv, c, tq=256):
    B, S, W3 = qkv.shape
    W = W3 // 3
    npair = W // LANES
    tq = min(tq, S)
    assert S % tq == 0 and W % LANES == 0, (S, tq, W)
    nq = S // tq
    below = [(i, t, 2 * tq) for i in range(nq) for t in range(i)]
    n_plain = len(below) - len(below) % 2
    masked = below[n_plain:] + [(i, i, 0) for i in range(nq)]
    masked += [(nq - 1, nq - 1, -2 * tq)] * (len(masked) % 2)
    items = below[:n_plain] + masked
    tabs = [jnp.asarray([it[n] for it in items], jnp.int32) for n in range(3)]
    seq = lambda col: pl.BlockSpec((1, S, LANES), lambda b, p, *_: (b, 0, col(p)))
    return pl.pallas_call(
        functools.partial(_fox_kernel, tq=tq, n_plain=n_plain, n_masked=len(masked),
                          prep_rows=tq * (2 if nq % 2 == 0 else 1), scale=HEAD_DIM ** -0.5),
        grid_spec=pltpu.PrefetchScalarGridSpec(
            num_scalar_prefetch=3,
            grid=(B, npair),
            in_specs=[seq(lambda p: p), seq(lambda p: npair + p), seq(lambda p: 2 * npair + p), seq(lambda p: 0)],
            out_specs=seq(lambda p: p),
            scratch_shapes=[pltpu.VMEM((2, S, LANES), BF16), pltpu.VMEM((2, S, LANES), BF16),
                            pltpu.VMEM((2, LANES, S), BF16),
                            pltpu.VMEM((2, 2, tq, tq), F32), pltpu.VMEM((2, 2, tq, tq), BF16),
                            pltpu.VMEM((nq, 2, LANES, tq), F32), pltpu.VMEM((nq, 2, 1, tq), F32)]),
        out_shape=jax.ShapeDtypeStruct((B, S, W), F32),
        compiler_params=_params(("parallel", "parallel")),
        name="fox_attn",
    )(*tabs, qkv, qkv, qkv, c)


def _rwkv_prep_kernel(u_ref, up_ref, mu_ref, w0_ref, a0_ref, kk_ref, ka_ref, rk_ref, wup_ref, aup_ref, gup_ref,
                      seg_ref, segt_ref,
                      r_o, lw_o, k_o, v_o, kkn_o, b_o, g_o, bonus_o, *, tm, width):
    i = pl.program_id(1)
    first_row = lax.broadcasted_iota(jnp.int32, (tm, 1), 0) == 0
    has_prev = i > 0

    def shifted(lo, hi):
        u = u_ref[0, :, lo:hi]
        prev_last = jnp.where(has_prev, up_ref[0, 7:8, lo:hi], 0.0)
        u_prev = jnp.where(first_row, prev_last, pltpu.roll(u, 1, 0))
        return u + mu_ref[:, lo:hi] * (u_prev - u)

    def head_sum(t):
        per_head = _split_dot(t, seg_ref[...], 2)
        return _split_dot(per_head, segt_ref[...], 2)

    W = width
    r = shifted(0, W)
    k = shifted(W, 2 * W)
    v = shifted(2 * W, 3 * W)
    wd = shifted(3 * W, 3 * W + LANES)
    ad = shifted(3 * W + LANES, 3 * W + 2 * LANES)
    gd = shifted(3 * W + 2 * LANES, 3 * W + 4 * LANES)

    wlin = w0_ref[...] + jnp.dot(jnp.tanh(wd).astype(BF16), wup_ref[...], preferred_element_type=F32)
    z = -wlin
    w = -(jnp.maximum(z, 0.0) + jnp.log(1.0 + jnp.exp(-jnp.abs(z)))) - 0.5
    lw_o[0] = -jnp.exp(w)
    a = jax.nn.sigmoid(a0_ref[...] + jnp.dot(ad.astype(BF16), aup_ref[...], preferred_element_type=F32))
    g = jnp.dot(jax.nn.sigmoid(gd).astype(BF16), gup_ref[...], preferred_element_type=F32)
    g_o[0] = g.astype(g_o.dtype)
    kk = k * kk_ref[...]
    kkn = kk / jnp.maximum(jnp.sqrt(head_sum(kk * kk)), 1e-12)
    k2 = k * (1.0 + (a - 1.0) * ka_ref[...])
    r_o[0] = r.astype(r_o.dtype)
    k_o[0] = k2.astype(k_o.dtype)
    v_o[0] = v.astype(v_o.dtype)
    kkn_o[0] = kkn.astype(kkn_o.dtype)
    b_o[0] = (a * kkn).astype(b_o.dtype)
    bonus_o[0] = (head_sum(r * k2 * rk_ref[...]) * v).astype(bonus_o.dtype)


def _rwkv_prep(u, mu, w0, a0, k_k, k_a, r_k, w_up, a_up, g_up, tm=256):
    B, S, UW = u.shape
    W = w0.shape[1]
    tm = min(tm, S)
    nh = W // HEAD_DIM
    head_of = jnp.arange(W) // HEAD_DIM
    seg = (head_of[:, None] == jnp.arange(LANES)[None, :]).astype(BF16)
    segt = seg.T
    del nh
    blk = lambda b, i: (b, i, 0)
    vec = lambda b, i: (0, 0)
    out = [jax.ShapeDtypeStruct((B, S, W), F32 if n == 1 else BF16) for n in range(8)]
    return pl.pallas_call(
        functools.partial(_rwkv_prep_kernel, tm=tm, width=W),
        grid=(B, S // tm),
        in_specs=[pl.BlockSpec((1, tm, UW), blk),
                  pl.BlockSpec((1, 8, UW), lambda b, i: (b, jnp.maximum(i * (tm // 8) - 1, 0), 0)),
                  pl.BlockSpec((1, UW), vec),
                  pl.BlockSpec((1, W), vec), pl.BlockSpec((1, W), vec), pl.BlockSpec((1, W), vec),
                  pl.BlockSpec((1, W), vec), pl.BlockSpec((1, W), vec),
                  pl.BlockSpec((LANES, W), vec), pl.BlockSpec((LANES, W), vec), pl.BlockSpec((2 * LANES, W), vec),
                  pl.BlockSpec((W, LANES), vec), pl.BlockSpec((LANES, W), vec)],
        out_specs=[pl.BlockSpec((1, tm, W), blk)] * 8,
        out_shape=out,
        compiler_params=_params(("parallel", "arbitrary")),
        name="rwkv_prep",
    )(u, u, mu, w0, a0, k_k, k_a, r_k, w_up, a_up, g_up, seg, segt)


def _rwkv_scan_kernel(r_ref, lw_ref, k_ref, v_ref, kkn_ref, b_ref, g_ref, bonus_ref, gng_ref, gnb_ref,
                      tri_ref, ones_ref, o_ref, s_ref, *, nchunk, npair):
    C = CHUNK

    @pl.when(pl.program_id(2) == 0)
    def _():
        s_ref[...] = jnp.zeros_like(s_ref)

    t_i = lax.broadcasted_iota(jnp.int32, (C, LANES), 0)
    lane = lax.broadcasted_iota(jnp.int32, (C, LANES), 1)
    s_i = lane & (HEAD_DIM - 1)
    low = lane < HEAD_DIM
    strict = s_i < t_i
    incl = s_i <= t_i
    eye = (s_i == t_i).astype(F32)
    levels = []
    k_ = 0
    while (1 << k_) < C:
        levels.append(((t_i >> k_) == ((s_i >> k_) + 1)) & ((t_i >> (k_ + 1)) == (s_i >> (k_ + 1))))
        k_ += 1
    blk_r = lax.broadcasted_iota(jnp.int32, (LANES, LANES), 0) // HEAD_DIM
    blk_c = lax.broadcasted_iota(jnp.int32, (LANES, LANES), 1) // HEAD_DIM
    same_head = blk_r == blk_c

    def block_diag(y):
        zero = jnp.zeros_like(y)
        return jnp.concatenate([jnp.where(low, y, zero), jnp.where(low, zero, y)], axis=0)

    def head_dot(x, y):
        return jnp.dot(x.astype(BF16), block_diag(y.astype(BF16)), preferred_element_type=F32)

    tri = tri_ref[...]
    ones_bd = ones_ref[...]

    def chunk(c, carry):
        sl = pl.ds(pl.multiple_of(c * C, C), C)
        pairs = range(npair)
        col = lambda t, p: t[:, p * LANES:(p + 1) * LANES]
        r = r_ref[0, sl, :]
        lw = lw_ref[0, sl, :]
        k = k_ref[0, sl, :]
        v = v_ref[0, sl, :]
        kkn = kkn_ref[0, sl, :]
        b = b_ref[0, sl, :]

        cum = _split_dot_rhs(tri, lw, 2)
        cum_last = cum[C - 1:C, :]
        g_inv = jnp.exp(-cum)
        g_tail = jnp.exp(cum_last - cum)
        a_t = (-kkn * jnp.exp(cum - lw)).astype(BF16)
        r_t = (r * jnp.exp(cum)).astype(BF16)
        k_t = (k * g_inv).astype(BF16)
        b_t = (b * g_inv).astype(BF16)
        k_tail = (k * g_tail).astype(BF16)
        b_tail = (b * g_tail).astype(BF16)
        v16 = v.astype(BF16)
        decay_last = jnp.exp(cum_last)

        s0 = [s_ref[p] for p in pairs]
        lhs = [jnp.concatenate([col(a_t, p), col(r_t, p)], axis=0) for p in pairs]
        amat = [_dot_nt(lhs[p], jnp.concatenate([block_diag(col(k_t, p)), block_diag(col(b_t, p))], axis=0))
                for p in pairs]
        s_proj = [_dot_nt(lhs[p], s0[p].astype(BF16)) for p in pairs]
        n_ab = [jnp.where(strict, amat[p][:C, LANES:], 0.0) for p in pairs]
        inv = [eye + jnp.where(levels[0], n_ab[p], 0.0) for p in pairs]
        for lvl in levels[1:]:
            x = [head_dot(inv[p], jnp.where(lvl, n_ab[p], 0.0)) for p in pairs]
            inv = [inv[p] + head_dot(x[p], inv[p]) for p in pairs]

        vbd = [block_diag(col(v16, p)) for p in pairs]
        rhs_u = [s_proj[p][:C] + jnp.dot(jnp.where(strict, amat[p][:C, :LANES], 0.0).astype(BF16), vbd[p],
                                         preferred_element_type=F32) for p in pairs]
        u = [head_dot(inv[p], rhs_u[p]) for p in pairs]
        u16 = [u[p].astype(BF16) for p in pairs]
        y = [s_proj[p][C:] + jnp.dot(jnp.where(jnp.concatenate([incl, incl], axis=1), amat[p][C:], 0.0).astype(BF16),
                                     jnp.concatenate([vbd[p], block_diag(u16[p])], axis=0),
                                     preferred_element_type=F32) for p in pairs]
        s_upd = [_dot_tn(jnp.concatenate([col(v16, p), u16[p]], axis=0),
                         jnp.concatenate([col(k_tail, p), col(b_tail, p)], axis=0)) for p in pairs]
        s_ref[...] = jnp.stack([s0[p] * col(decay_last, p) + jnp.where(same_head, s_upd[p], 0.0) for p in pairs],
                               axis=0)

        mean = [_split_dot(y[p], ones_bd, 2) * (1.0 / HEAD_DIM) for p in pairs]
        yc = [y[p] - mean[p] for p in pairs]
        var = [_split_dot(yc[p] * yc[p], ones_bd, 2) * (1.0 / HEAD_DIM) for p in pairs]
        yn = jnp.concatenate([yc[p] * lax.rsqrt(var[p] + GN_EPS) for p in pairs], axis=1)
        o_ref[0, sl, :] = ((yn * gng_ref[...] + gnb_ref[...] + bonus_ref[0, sl, :]) * g_ref[0, sl, :]).astype(o_ref.dtype)
        return carry

    lax.fori_loop(0, nchunk, chunk, 0)


def _rwkv_scan(r, lw, k, v, kkn, b, g, bonus, gn_g, gn_b, tc=256, npair=8):
    B, S, W = r.shape
    tc = min(tc, S)
    npair = min(npair, W // LANES)
    assert S % tc == 0 and tc % CHUNK == 0 and W % (npair * LANES) == 0, (S, tc, W, npair)
    C = CHUNK
    tri = jnp.tril(jnp.ones((C, C), F32)).astype(BF16)
    blk_id = jnp.arange(LANES) // HEAD_DIM
    ones_bd = (blk_id[:, None] == blk_id[None, :]).astype(BF16)
    wl = npair * LANES
    seq = pl.BlockSpec((1, tc, wl), lambda bb, p, t: (bb, t, p))
    vec = pl.BlockSpec((1, wl), lambda bb, p, t: (0, p))
    return pl.pallas_call(
        functools.partial(_rwkv_scan_kernel, nchunk=tc // C, npair=npair),
        grid=(B, W // wl, S // tc),
        in_specs=[seq] * 8 + [vec, vec,
                              pl.BlockSpec((C, C), lambda bb, p, t: (0, 0)),
                              pl.BlockSpec((LANES, LANES), lambda bb, p, t: (0, 0))],
        out_specs=seq,
        out_shape=jax.ShapeDtypeStruct((B, S, W), BF16),
        scratch_shapes=[pltpu.VMEM((npair, LANES, LANES), F32)],
        compiler_params=_params(("parallel", "parallel", "arbitrary")),
        name="rwkv_scan",
    )(r, lw, k, v, kkn, b, g, bonus, gn_g, gn_b, tri, ones_bd)


def _mix_kernel(fo_ref, ro_ref, x_ref, wf_ref, wr_ref, fg_ref, g_ref, b_ref, o_ref, *, alpha):
    fo = fo_ref[...]
    fo = fo * lax.rsqrt(jnp.mean(fo * fo, axis=-1, keepdims=True) + FOX_EPS) * fg_ref[...]
    mix = jnp.dot(fo.astype(BF16), wf_ref[...], preferred_element_type=F32)
    mix = mix + jnp.dot(ro_ref[...].astype(BF16), wr_ref[...], preferred_element_type=F32)
    o_ref[...] = _layer_norm(alpha * x_ref[...] + mix, g_ref[...], b_ref[...])


def _mix(fo, ro, x, w_mix, layer, fox_g, g, b, alpha, tm=512):
    T, D = x.shape
    WF, WR = fo.shape[1], ro.shape[1]
    tm = min(tm, T)
    assert WF == WR and T % tm == 0, (WF, WR, T, tm)
    row = lambda i: (i, 0)
    vec = lambda i: (0, 0)
    return pl.pallas_call(
        functools.partial(_mix_kernel, alpha=alpha),
        grid=(T // tm,),
        in_specs=[pl.BlockSpec((tm, WF), row), pl.BlockSpec((tm, WR), row), pl.BlockSpec((tm, D), row),
                  pl.BlockSpec((None, WF, D), lambda i: (layer, 0, 0)),
                  pl.BlockSpec((None, WR, D), lambda i: (layer, 1, 0)),
                  pl.BlockSpec((1, WF), vec), pl.BlockSpec((1, D), vec), pl.BlockSpec((1, D), vec)],
        out_specs=pl.BlockSpec((tm, D), row),
        out_shape=jax.ShapeDtypeStruct((T, D), F32),
        compiler_params=_params(("parallel",)),
        name="mix_out",
    )(fo, ro, x, w_mix, w_mix, fox_g, g, b)


def _xattn_kernel(x_ref, wq_ref, kv_k_ref, kv_v_ref, wo_ref, g_ref, b_ref, o_ref, *, alpha, heads):
    x = x_ref[0]
    D = x.shape[1]
    dh = D // heads
    q = jnp.dot(x.astype(BF16), wq_ref[...], preferred_element_type=F32).astype(BF16)
    outs = []
    for h in range(heads):
        cols = slice(h * dh, (h + 1) * dh)
        s = _dot_nt(q[:, cols], kv_k_ref[0, :, cols]) * (dh ** -0.5)
        s = s - jnp.max(s, axis=1, keepdims=True)
        e = jnp.exp(s)
        p = e / jnp.sum(e, axis=1, keepdims=True)
        outs.append(jnp.dot(p.astype(BF16), kv_v_ref[0, :, cols], preferred_element_type=F32).astype(BF16))
    o = jnp.concatenate(outs, axis=1)
    xa = jnp.dot(o, wo_ref[...], preferred_element_type=F32)
    o_ref[0] = _layer_norm(alpha * x + xa, g_ref[...], b_ref[...])


def _xattn(x, kv, wq, wo, layer, g, b, alpha, tm=512):
    B, S, D = x.shape
    M = kv.shape[1]
    tm = min(tm, S)
    vec = lambda bb, i: (0, 0)
    return pl.pallas_call(
        functools.partial(_xattn_kernel, alpha=alpha, heads=XA_HEADS),
        grid=(B, S // tm),
        in_specs=[pl.BlockSpec((1, tm, D), lambda bb, i: (bb, i, 0)),
                  pl.BlockSpec((None, D, D), lambda bb, i: (layer, 0, 0), pipeline_mode=pl.Buffered(1)),
                  pl.BlockSpec((1, M, D), lambda bb, i: (bb, 0, 0)),
                  pl.BlockSpec((1, M, D), lambda bb, i: (bb, 0, 1)),
                  pl.BlockSpec((None, D, D), lambda bb, i: (layer, 0, 0), pipeline_mode=pl.Buffered(1)),
                  pl.BlockSpec((1, D), vec), pl.BlockSpec((1, D), vec)],
        out_specs=pl.BlockSpec((1, tm, D), lambda bb, i: (bb, i, 0)),
        out_shape=jax.ShapeDtypeStruct((B, S, D), F32),
        compiler_params=_params(("parallel", "arbitrary")),
        name="xattn",
    )(x, wq, kv, kv, wo, g, b)


def _pad_cols(w, n):
    return jnp.pad(w, ((0, 0), (0, n - w.shape[1])))


def _pad_rows(w, n):
    return jnp.pad(w, ((0, n - w.shape[0]), (0, 0)))


def kernel(x, mem, ln_g, ln_b, ffn1_w13, ffn1_w2, w_in, fox_f_bias, fox_out_g, rwkv_mu, rwkv_w0, rwkv_w_up, rwkv_a0, rwkv_a_up, rwkv_g_up, rwkv_k_k, rwkv_k_a, rwkv_r_k, rwkv_gn_g, rwkv_gn_b, w_mix_out, xa_wq, xa_wkv, xa_wo, ffn2_w13, ffn2_w2):
    B, S, D = x.shape
    depth = ln_g.shape[0]
    M = mem.shape[1]
    alpha = (2.0 * depth) ** 0.25
    fox_heads = fox_f_bias.shape[1]
    WF = fox_heads * HEAD_DIM
    WR = rwkv_w0.shape[1]
    decay_lora, aaa_lora, gate_lora = rwkv_w_up.shape[1], rwkv_a_up.shape[1], rwkv_g_up.shape[1]
    c1 = 3 * WF
    c2 = c1 + fox_heads
    T = B * S
    row = lambda t: t.reshape(1, -1)

    h = x.reshape(T, D)
    mem_b = mem.reshape(B * M, D).astype(BF16)
    ffn1_w13, ffn1_w2, ffn2_w13, ffn2_w2, w_mix_out, xa_wq, xa_wkv, xa_wo = (
        w.astype(BF16) for w in (ffn1_w13, ffn1_w2, ffn2_w13, ffn2_w2, w_mix_out, xa_wq, xa_wkv, xa_wo))
    for l in range(depth):
        h, hb = _ffn(h, ffn1_w13, ffn1_w2, l, row(ln_g[l, 0]), row(ln_b[l, 0]), alpha)

        wl = w_in[l].astype(BF16)
        w_fox = wl[:, :c1]
        w_gate = _pad_cols(wl[:, c1:c2], LANES)
        o = c2 + 3 * WR
        w_rw = jnp.concatenate([wl[:, c2:o],
                                _pad_cols(wl[:, o:o + decay_lora], LANES),
                                _pad_cols(wl[:, o + decay_lora:o + decay_lora + aaa_lora], LANES),
                                _pad_cols(wl[:, o + decay_lora + aaa_lora:], 2 * LANES)], axis=1)
        mu = rwkv_mu[l]
        o = 3 * WR
        mu_p = jnp.concatenate([mu[:o],
                                jnp.pad(mu[o:o + decay_lora], (0, LANES - decay_lora)),
                                jnp.pad(mu[o + decay_lora:o + decay_lora + aaa_lora], (0, LANES - aaa_lora)),
                                jnp.pad(mu[o + decay_lora + aaa_lora:], (0, 2 * LANES - gate_lora))])

        qkv = _matmul(hb, w_fox, BF16).reshape(B, S, c1)
        u = _matmul(hb, w_rw, F32).reshape(B, S, -1)
        bias = jnp.pad(fox_f_bias[l], (0, LANES - fox_heads)).reshape(1, LANES)
        fo = _fox_attention(qkv, _gate_cumsum(hb.reshape(B, S, D), w_gate, bias))

        r, lw, k2, v, kkn, bb, g, bonus = _rwkv_prep(
            u, row(mu_p), row(rwkv_w0[l]), row(rwkv_a0[l]), row(rwkv_k_k[l]), row(rwkv_k_a[l]),
            row(rwkv_r_k[l]),
            _pad_rows(rwkv_w_up[l], LANES).astype(BF16), _pad_rows(rwkv_a_up[l], LANES).astype(BF16),
            _pad_rows(rwkv_g_up[l], 2 * LANES).astype(BF16))
        ro = _rwkv_scan(r, lw, k2, v, kkn, bb, g, bonus, row(rwkv_gn_g[l]), row(rwkv_gn_b[l]))

        h = _mix(fo.reshape(T, WF), ro.reshape(T, WR), h, w_mix_out, l, row(fox_out_g[l]),
                 row(ln_g[l, 1]), row(ln_b[l, 1]), alpha)

        kv = _matmul(mem_b, xa_wkv, BF16, layer=l).reshape(B, M, 2 * D)
        h = _xattn(h.reshape(B, S, D), kv, xa_wq, xa_wo, l,
                   row(ln_g[l, 2]), row(ln_b[l, 2]), alpha).reshape(T, D)

        h, _ = _ffn(h, ffn2_w13, ffn2_w2, l, row(ln_g[l, 3]), row(ln_b[l, 3]), alpha)
    return h.reshape(B, S, D)
```

```python
import functools

import jax
import jax.numpy as jnp
from jax import lax
from jax.experimental import pallas as pl
from jax.experimental.pallas import tpu as pltpu

HEAD_DIM = 64
XA_HEADS = 4
LN_EPS = 1e-5
GN_EPS = 64e-5
FOX_EPS = 1e-6
NEG_INF = -1e30
LOG2_E = 1.4426950408889634
LANES = 128
CHUNK = 64
FOX_BEATS_PER_TRIP = 4
VMEM_LIMIT = 56 * 1024 * 1024

F32 = jnp.float32
BF16 = jnp.bfloat16


def _params(semantics):
    return pltpu.CompilerParams(dimension_semantics=semantics, vmem_limit_bytes=VMEM_LIMIT)


def _layer_norm(z, g, b):
    mu = jnp.mean(z, axis=-1, keepdims=True)
    zc = z - mu
    var = jnp.mean(zc * zc, axis=-1, keepdims=True)
    return zc * lax.rsqrt(var + LN_EPS) * g + b


def _split_dot(a, b01, terms):
    acc = None
    rem = a
    for _ in range(terms):
        part = rem.astype(BF16)
        d = jnp.dot(part, b01, preferred_element_type=F32)
        acc = d if acc is None else acc + d
        rem = rem - part.astype(F32)
    return acc


def _split_dot_rhs(a01, b, terms):
    acc = None
    rem = b
    for _ in range(terms):
        part = rem.astype(BF16)
        d = jnp.dot(a01, part, preferred_element_type=F32)
        acc = d if acc is None else acc + d
        rem = rem - part.astype(F32)
    return acc


def _dot_nt(a, b):
    return lax.dot_general(a, b, (((1,), (1,)), ((), ())), preferred_element_type=F32)


def _dot_tn(a, b):
    return lax.dot_general(a, b, (((0,), (0,)), ((), ())), preferred_element_type=F32)


def _ffn_kernel(x_ref, wg_ref, wu_ref, w2_ref, g_ref, b_ref, o_ref, ob_ref, acc_ref, xb_ref, *, alpha):
    j = pl.program_id(1)

    @pl.when(j == 0)
    def _():
        xb_ref[...] = x_ref[...].astype(BF16)
        acc_ref[...] = jnp.zeros_like(acc_ref)

    xb = xb_ref[...]
    gate = jnp.dot(xb, wg_ref[...], preferred_element_type=F32)
    up = jnp.dot(xb, wu_ref[...], preferred_element_type=F32)
    h = (gate * jax.nn.sigmoid(gate) * up).astype(BF16)
    acc_ref[...] += jnp.dot(h, w2_ref[...], preferred_element_type=F32)

    @pl.when(j == pl.num_programs(1) - 1)
    def _():
        y = _layer_norm(alpha * x_ref[...] + 0.5 * acc_ref[...], g_ref[...], b_ref[...])
        o_ref[...] = y
        ob_ref[...] = y.astype(BF16)


def _ffn(x, w13, w2, layer, g, b, alpha, tm=512, tf=512):
    T, D = x.shape
    F = w2.shape[1]
    tm, tf = min(tm, T), min(tf, F)
    assert T % tm == 0 and F % tf == 0, (T, tm, F, tf)
    nf = F // tf
    row = lambda i, j: (i, 0)
    vec = lambda i, j: (0, 0)
    return pl.pallas_call(
        functools.partial(_ffn_kernel, alpha=alpha),
        grid=(T // tm, nf),
        in_specs=[pl.BlockSpec((tm, D), row),
                  pl.BlockSpec((None, D, tf), lambda i, j: (layer, 0, j)),
                  pl.BlockSpec((None, D, tf), lambda i, j: (layer, 0, j + nf)),
                  pl.BlockSpec((None, tf, D), lambda i, j: (layer, j, 0)),
                  pl.BlockSpec((1, D), vec),
                  pl.BlockSpec((1, D), vec)],
        out_specs=[pl.BlockSpec((tm, D), row), pl.BlockSpec((tm, D), row)],
        out_shape=[jax.ShapeDtypeStruct((T, D), F32), jax.ShapeDtypeStruct((T, D), BF16)],
        scratch_shapes=[pltpu.VMEM((tm, D), F32), pltpu.VMEM((tm, D), BF16)],
        compiler_params=_params(("parallel", "arbitrary")),
        name="ffn",
    )(x, w13, w13, w2, g, b)


def _mm_kernel(x_ref, w_ref, o_ref):
    o_ref[...] = jnp.dot(x_ref[...], w_ref[...], preferred_element_type=F32).astype(o_ref.dtype)


def _matmul(x, w, out_dtype, layer=None, tm=1024, tn_max=1792):
    T, K = x.shape
    N = w.shape[-1]
    tm = min(tm, T)
    assert T % tm == 0, (T, tm)
    tn = max(c for c in range(LANES, min(tn_max, N) + 1, LANES) if N % c == 0)
    return pl.pallas_call(
        _mm_kernel,
        grid=(T // tm, N // tn),
        in_specs=[pl.BlockSpec((tm, K), lambda i, j: (i, 0)),
                  pl.BlockSpec((K, tn), lambda i, j: (0, j)) if layer is None else
                  pl.BlockSpec((None, K, tn), lambda i, j: (layer, 0, j))],
        out_specs=pl.BlockSpec((tm, tn), lambda i, j: (i, j)),
        out_shape=jax.ShapeDtypeStruct((T, N), out_dtype),
        compiler_params=_params(("parallel", "arbitrary")),
        name="proj",
    )(x, w)


def _gate_kernel(x_ref, w_ref, bias_ref, tri_ref, c_ref, carry_ref):
    @pl.when(pl.program_id(1) == 0)
    def _():
        carry_ref[...] = jnp.zeros_like(carry_ref)

    z = jnp.dot(x_ref[0], w_ref[...], preferred_element_type=F32) + bias_ref[...]
    logf = jnp.minimum(z, 0.0) - jnp.log(1.0 + jnp.exp(-jnp.abs(z)))
    c = _split_dot_rhs(tri_ref[...], logf, 3) + carry_ref[...]
    c_ref[0] = c
    ts = c.shape[0]
    carry_ref[...] = c[ts - 1:ts, :]


def _gate_cumsum(xb, w_gate, bias, ts=512):
    B, S, D = xb.shape
    ts = min(ts, S)
    assert S % ts == 0, (S, ts)
    tri = jnp.tril(jnp.ones((ts, ts), F32)).astype(BF16)
    return pl.pallas_call(
        _gate_kernel,
        grid=(B, S // ts),
        in_specs=[pl.BlockSpec((1, ts, D), lambda b, s: (b, s, 0)),
                  pl.BlockSpec((D, LANES), lambda b, s: (0, 0)),
                  pl.BlockSpec((1, LANES), lambda b, s: (0, 0)),
                  pl.BlockSpec((ts, ts), lambda b, s: (0, 0))],
        out_specs=pl.BlockSpec((1, ts, LANES), lambda b, s: (b, s, 0)),
        out_shape=jax.ShapeDtypeStruct((B, S, LANES), F32),
        scratch_shapes=[pltpu.VMEM((1, LANES), F32)],
        compiler_params=_params(("parallel", "arbitrary")),
        name="fox_gate",
    )(xb, w_gate, bias, tri)


def _fox_kernel(qblk_tab, tile_tab, thr_tab, q_ref, k_ref, v_ref, cc_ref, o_ref,
                kp_ref, qp_ref, vt_ref, s_scr, p_scr, acc_scr, m_scr, *, tq, n_plain, n_masked, prep_rows, scale):
    p = pl.program_id(1)
    S = k_ref.shape[1]

    def build_operands():
        rows = lax.broadcasted_iota(jnp.int32, (LANES, LANES), 0)
        cols = lax.broadcasted_iota(jnp.int32, (LANES, LANES), 1)
        ident = (rows == cols).astype(BF16)
        lane = lax.broadcasted_iota(jnp.int32, (prep_rows, LANES), 1)

        def prep(n, carry):
            sl = pl.ds(pl.multiple_of(n * prep_rows, prep_rows), prep_rows)
            kk = k_ref[0, sl, :]
            vv = v_ref[0, sl, :]
            cc = cc_ref[0, sl, :] * LOG2_E
            c1 = cc.astype(BF16)
            rem = cc - c1.astype(F32)
            c2 = rem.astype(BF16)
            c3 = (rem - c2.astype(F32)).astype(BF16)
            placed = None
            for t, part in enumerate((c1, c2, c3)):
                sel = jnp.zeros((LANES, LANES), F32)
                for e in range(2):
                    base = HEAD_DIM * (1 - e)
                    mine = rows == 2 * p + e
                    sel = sel + (mine & (cols == base + t)).astype(F32) - (mine & (cols == base + 3 + t)).astype(F32)
                d = jnp.dot(part, sel.astype(BF16), preferred_element_type=F32)
                placed = d if placed is None else placed + d
            pos = lane & (HEAD_DIM - 1)
            first3 = pos < 3
            next3 = (pos >= 3) & (pos < 6)
            k_bias = jnp.where(first3, 1.0, jnp.where(next3, placed, 0.0)).astype(BF16)
            q_bias = jnp.where(first3, placed, jnp.where(next3, 1.0, 0.0)).astype(BF16)
            qq = q_ref[0, sl, :] * (scale * LOG2_E)
            for e in range(2):
                in_head = (lane >= HEAD_DIM * e) & (lane < HEAD_DIM * (e + 1))
                kp_ref[e, sl, :] = jnp.where(in_head, kk, k_bias)
                qp_ref[e, sl, :] = jnp.where(in_head, qq, q_bias)
                v_aug = jnp.where(in_head, vv, jnp.ones_like(vv))
                vt_ref[e, :, sl] = _dot_nt(ident, v_aug).astype(BF16)
            return carry

        lax.fori_loop(0, S // prep_rows, prep, 0)

    build_operands()

    key_minus_qry = (lax.broadcasted_iota(jnp.int32, (tq, tq), 0) - lax.broadcasted_iota(jnp.int32, (tq, tq), 1))
    head_row = lax.broadcasted_iota(jnp.int32, (LANES, tq), 0)
    heads = range(2)

    def rows_of(block):
        return pl.ds(pl.multiple_of(block * tq, tq), tq)

    def scores(n, slot, masked):
        s = [_dot_nt(kp_ref[e, rows_of(tile_tab[n]), :], qp_ref[e, rows_of(qblk_tab[n]), :]) for e in heads]
        if masked:
            keep = key_minus_qry <= thr_tab[n]
            s = [jnp.where(keep, s[e], NEG_INF) for e in heads]
        for e in heads:
            s_scr[slot, e] = s[e]
        return [jnp.max(s[e], axis=0, keepdims=True) for e in heads]

    def probabilities(n, slot, tile_max):
        blk = qblk_tab[n]
        rescale = []
        for e in heads:
            m_prev = m_scr[blk, e]
            m_new = jnp.maximum(m_prev, tile_max[e])
            m_scr[blk, e] = m_new
            p_scr[slot, e] = jnp.exp2(s_scr[slot, e] - m_new).astype(BF16)
            rescale.append(jnp.exp2(m_prev - m_new))
        return rescale

    def accumulate(n, slot, rescale):
        blk = qblk_tab[n]
        for e in heads:
            acc_scr[blk, e] = rescale[e] * acc_scr[blk, e] + jnp.dot(vt_ref[e, :, rows_of(tile_tab[n])],
                                                                     p_scr[slot, e], preferred_element_type=F32)

    def run_items(first, count, masked):
        def beat(n, slot, tile_max, rescale, stages=3):
            accumulate(n, slot, rescale)
            if stages >= 2:
                rescale = probabilities(n + 1, 1 - slot, tile_max)
            if stages >= 3:
                tile_max = scores(n + 2, slot, masked)
            return tile_max, rescale

        def body(jj, carry):
            tile_max, rescale = list(carry[:2]), list(carry[2:])
            for k in range(FOX_BEATS_PER_TRIP):
                tile_max, rescale = beat(first + FOX_BEATS_PER_TRIP * jj + k, k % 2, tile_max, rescale)
            return tuple(tile_max + rescale)

        rescale = probabilities(first, 0, scores(first, 0, masked))
        tile_max = scores(first + 1, 1, masked)
        trips, rest = divmod(count - 2, FOX_BEATS_PER_TRIP)
        carry = lax.fori_loop(0, trips, body, tuple(tile_max + rescale))
        tile_max, rescale = list(carry[:2]), list(carry[2:])
        done = trips * FOX_BEATS_PER_TRIP
        for k in range(rest):
            tile_max, rescale = beat(first + done + k, k % 2, tile_max, rescale)
        tile_max, rescale = beat(first + count - 2, 0, tile_max, rescale, stages=2)
        beat(first + count - 1, 1, tile_max, rescale, stages=1)

    acc_scr[...] = jnp.zeros_like(acc_scr)
    m_scr[...] = jnp.full_like(m_scr, NEG_INF)
    if n_plain:
        run_items(0, n_plain, False)
    run_items(n_plain, n_masked, True)

    def finish(blk, carry):
        out = [acc_scr[blk, e] / acc_scr[blk, e, HEAD_DIM * (1 - e):HEAD_DIM * (1 - e) + 1, :] for e in heads]
        o_ref[0, rows_of(blk), :] = jnp.where(head_row < HEAD_DIM, out[0], out[1]).T
        return carry

    lax.fori_loop(0, S // tq, finish, 0)


def _fox_attention(qkv, c, tq=512):
    B, S, W3 = qkv.shape
    W = W3 // 3
    npair = W // LANES
    tq = min(tq, S)
    assert S % tq == 0 and W % LANES == 0, (S, tq, W)
    nq = S // tq
    below = [(i, t, 2 * tq) for i in range(nq) for t in range(i)]
    n_plain = len(below) - len(below) % 2
    masked = below[n_plain:] + [(i, i, 0) for i in range(nq)]
    masked += [(nq - 1, nq - 1, -2 * tq)] * (len(masked) % 2)
    items = below[:n_plain] + masked
    tabs = [jnp.asarray([it[n] for it in items], jnp.int32) for n in range(3)]
    seq = lambda col: pl.BlockSpec((1, S, LANES), lambda b, p, *_: (b, 0, col(p)))
    return pl.pallas_call(
        functools.partial(_fox_kernel, tq=tq, n_plain=n_plain, n_masked=len(masked),
                          prep_rows=tq * (2 if nq % 2 == 0 else 1), scale=HEAD_DIM ** -0.5),
        grid_spec=pltpu.PrefetchScalarGridSpec(
            num_scalar_prefetch=3,
            grid=(B, npair),
            in_specs=[seq(lambda p: p), seq(lambda p: npair + p), seq(lambda p: 2 * npair + p), seq(lambda p: 0)],
            out_specs=seq(lambda p: p),
            scratch_shapes=[pltpu.VMEM((2, S, LANES), BF16), pltpu.VMEM((2, S, LANES), BF16),
                            pltpu.VMEM((2, LANES, S), BF16),
                            pltpu.VMEM((2, 2, tq, tq), F32), pltpu.VMEM((2, 2, tq, tq), BF16),
                            pltpu.VMEM((nq, 2, LANES, tq), F32), pltpu.VMEM((nq, 2, 1, tq), F32)]),
        out_shape=jax.ShapeDtypeStruct((B, S, W), F32),
        compiler_params=_params(("parallel", "parallel")),
        name="fox_attn",
    )(*tabs, qkv, qkv, qkv, c)


def _rwkv_prep_kernel(u_ref, up_ref, mu_ref, w0_ref, a0_ref, kk_ref, ka_ref, rk_ref, wup_ref, aup_ref, gup_ref,
                      seg_ref, segt_ref,
                      r_o, lw_o, k_o, v_o, kkn_o, b_o, g_o, bonus_o, *, tm, width):
    i = pl.program_id(1)
    first_row = lax.broadcasted_iota(jnp.int32, (tm, 1), 0) == 0
    has_prev = i > 0

    def shifted(lo, hi):
        u = u_ref[0, :, lo:hi]
        prev_last = jnp.where(has_prev, up_ref[0, 7:8, lo:hi], 0.0)
        u_prev = jnp.where(first_row, prev_last, pltpu.roll(u, 1, 0))
        return u + mu_ref[:, lo:hi] * (u_prev - u)

    def head_sum(t):
        per_head = _split_dot(t, seg_ref[...], 2)
        return _split_dot(per_head, segt_ref[...], 2)

    W = width
    r = shifted(0, W)
    k = shifted(W, 2 * W)
    v = shifted(2 * W, 3 * W)
    wd = shifted(3 * W, 3 * W + LANES)
    ad = shifted(3 * W + LANES, 3 * W + 2 * LANES)
    gd = shifted(3 * W + 2 * LANES, 3 * W + 4 * LANES)

    wlin = w0_ref[...] + jnp.dot(jnp.tanh(wd).astype(BF16), wup_ref[...], preferred_element_type=F32)
    z = -wlin
    w = -(jnp.maximum(z, 0.0) + jnp.log(1.0 + jnp.exp(-jnp.abs(z)))) - 0.5
    lw_o[0] = -jnp.exp(w)
    a = jax.nn.sigmoid(a0_ref[...] + jnp.dot(ad.astype(BF16), aup_ref[...], preferred_element_type=F32))
    g = jnp.dot(jax.nn.sigmoid(gd).astype(BF16), gup_ref[...], preferred_element_type=F32)
    g_o[0] = g.astype(g_o.dtype)
    kk = k * kk_ref[...]
    kkn = kk / jnp.maximum(jnp.sqrt(head_sum(kk * kk)), 1e-12)
    k2 = k * (1.0 + (a - 1.0) * ka_ref[...])
    r_o[0] = r.astype(r_o.dtype)
    k_o[0] = k2.astype(k_o.dtype)
    v_o[0] = v.astype(v_o.dtype)
    kkn_o[0] = kkn.astype(kkn_o.dtype)
    b_o[0] = (a * kkn).astype(b_o.dtype)
    bonus_o[0] = (head_sum(r * k2 * rk_ref[...]) * v).astype(bonus_o.dtype)


def _rwkv_prep(u, mu, w0, a0, k_k, k_a, r_k, w_up, a_up, g_up, tm=256):
    B, S, UW = u.shape
    W = w0.shape[1]
    tm = min(tm, S)
    nh = W // HEAD_DIM
    head_of = jnp.arange(W) // HEAD_DIM
    seg = (head_of[:, None] == jnp.arange(LANES)[None, :]).astype(BF16)
    segt = seg.T
    del nh
    blk = lambda b, i: (b, i, 0)
    vec = lambda b, i: (0, 0)
    out = [jax.ShapeDtypeStruct((B, S, W), F32 if n == 1 else BF16) for n in range(8)]
    return pl.pallas_call(
        functools.partial(_rwkv_prep_kernel, tm=tm, width=W),
        grid=(B, S // tm),
        in_specs=[pl.BlockSpec((1, tm, UW), blk),
                  pl.BlockSpec((1, 8, UW), lambda b, i: (b, jnp.maximum(i * (tm // 8) - 1, 0), 0)),
                  pl.BlockSpec((1, UW), vec),
                  pl.BlockSpec((1, W), vec), pl.BlockSpec((1, W), vec), pl.BlockSpec((1, W), vec),
                  pl.BlockSpec((1, W), vec), pl.BlockSpec((1, W), vec),
                  pl.BlockSpec((LANES, W), vec), pl.BlockSpec((LANES, W), vec), pl.BlockSpec((2 * LANES, W), vec),
                  pl.BlockSpec((W, LANES), vec), pl.BlockSpec((LANES, W), vec)],
        out_specs=[pl.BlockSpec((1, tm, W), blk)] * 8,
        out_shape=out,
        compiler_params=_params(("parallel", "arbitrary")),
        name="rwkv_prep",
    )(u, u, mu, w0, a0, k_k, k_a, r_k, w_up, a_up, g_up, seg, segt)


def _rwkv_scan_kernel(r_ref, lw_ref, k_ref, v_ref, kkn_ref, b_ref, g_ref, bonus_ref, gng_ref, gnb_ref,
                      tri_ref, o_ref, s_ref, *, nchunk, npair, nbatch):
    C = CHUNK

    @pl.when(pl.program_id(2) == 0)
    def _():
        s_ref[...] = jnp.zeros_like(s_ref)

    t_i = lax.broadcasted_iota(jnp.int32, (C, LANES), 0)
    lane = lax.broadcasted_iota(jnp.int32, (C, LANES), 1)
    s_i = lane & (HEAD_DIM - 1)
    low = lane < HEAD_DIM
    strict = s_i < t_i
    incl = s_i <= t_i
    eye = (s_i == t_i).astype(F32)
    levels = []
    k_ = 0
    while (1 << k_) < C:
        levels.append(((t_i >> k_) == ((s_i >> k_) + 1)) & ((t_i >> (k_ + 1)) == (s_i >> (k_ + 1))))
        k_ += 1
    blk_r = lax.broadcasted_iota(jnp.int32, (LANES, LANES), 0) // HEAD_DIM
    blk_c = lax.broadcasted_iota(jnp.int32, (LANES, LANES), 1) // HEAD_DIM
    same_head = blk_r == blk_c

    def block_diag(y):
        zero = jnp.zeros_like(y)
        return jnp.concatenate([jnp.where(low, y, zero), jnp.where(low, zero, y)], axis=0)

    def head_dot(x, y):
        return jnp.dot(x.astype(BF16), block_diag(y.astype(BF16)), preferred_element_type=F32)

    def head_mean(t):
        lo = jnp.sum(jnp.where(low, t, 0.0), axis=1, keepdims=True)
        hi = jnp.sum(jnp.where(low, 0.0, t), axis=1, keepdims=True)
        return jnp.where(low, lo, hi) * (1.0 / HEAD_DIM)

    tri = tri_ref[...]

    def chunk(c, carry):
        sl = pl.ds(pl.multiple_of(c * C, C), C)
        pairs = range(nbatch * npair)
        col = lambda t, p: t[:, p * LANES:(p + 1) * LANES]
        rows = lambda ref: jnp.concatenate([ref[bb, sl, :] for bb in range(nbatch)], axis=1)
        r = rows(r_ref)
        lw = rows(lw_ref)
        k = rows(k_ref)
        v = rows(v_ref)
        kkn = rows(kkn_ref)
        b = rows(b_ref)

        cum = _split_dot_rhs(tri, lw, 2)
        cum_last = cum[C - 1:C, :]
        g_inv = jnp.exp(-cum)
        g_tail = jnp.exp(cum_last - cum)
        a_t = (-kkn * jnp.exp(cum - lw)).astype(BF16)
        r_t = (r * jnp.exp(cum)).astype(BF16)
        k_t = (k * g_inv).astype(BF16)
        b_t = (b * g_inv).astype(BF16)
        k_tail = (k * g_tail).astype(BF16)
        b_tail = (b * g_tail).astype(BF16)
        v16 = v.astype(BF16)
        decay_last = jnp.exp(cum_last)

        s0 = [s_ref[p] for p in pairs]
        lhs = [jnp.concatenate([col(a_t, p), col(r_t, p)], axis=0) for p in pairs]
        amat = [_dot_nt(lhs[p], jnp.concatenate([block_diag(col(k_t, p)), block_diag(col(b_t, p))], axis=0))
                for p in pairs]
        s_proj = [_dot_nt(lhs[p], s0[p].astype(BF16)) for p in pairs]
        n_ab = [jnp.where(strict, amat[p][:C, LANES:], 0.0) for p in pairs]
        inv = [eye + jnp.where(levels[0], n_ab[p], 0.0) for p in pairs]
        for lvl in levels[1:]:
            x = [head_dot(inv[p], jnp.where(lvl, n_ab[p], 0.0)) for p in pairs]
            inv = [inv[p] + head_dot(x[p], inv[p]) for p in pairs]

        vbd = [block_diag(col(v16, p)) for p in pairs]
        rhs_u = [s_proj[p][:C] + jnp.dot(jnp.where(strict, amat[p][:C, :LANES], 0.0).astype(BF16), vbd[p],
                                         preferred_element_type=F32) for p in pairs]
        u = [head_dot(inv[p], rhs_u[p]) for p in pairs]
        u16 = [u[p].astype(BF16) for p in pairs]
        y = [s_proj[p][C:] + jnp.dot(jnp.where(jnp.concatenate([incl, incl], axis=1), amat[p][C:], 0.0).astype(BF16),
                                     jnp.concatenate([vbd[p], block_diag(u16[p])], axis=0),
                                     preferred_element_type=F32) for p in pairs]
        s_upd = [_dot_tn(jnp.concatenate([col(v16, p), u16[p]], axis=0),
                         jnp.concatenate([col(k_tail, p), col(b_tail, p)], axis=0)) for p in pairs]
        s_ref[...] = jnp.stack([s0[p] * col(decay_last, p) + jnp.where(same_head, s_upd[p], 0.0) for p in pairs],
                               axis=0)

        yc = [y[p] - head_mean(y[p]) for p in pairs]
        var = [head_mean(yc[p] * yc[p]) for p in pairs]
        yn = jnp.concatenate([yc[p] * lax.rsqrt(var[p] + GN_EPS) for p in pairs], axis=1)
        gn_g = jnp.concatenate([gng_ref[...]] * nbatch, axis=1)
        gn_b = jnp.concatenate([gnb_ref[...]] * nbatch, axis=1)
        out = ((yn * gn_g + gn_b + rows(bonus_ref)) * rows(g_ref)).astype(o_ref.dtype)
        width = npair * LANES
        for bb in range(nbatch):
            o_ref[bb, sl, :] = out[:, bb * width:(bb + 1) * width]
        return carry

    lax.fori_loop(0, nchunk, chunk, 0)


def _rwkv_scan(r, lw, k, v, kkn, b, g, bonus, gn_g, gn_b, tc=256, npair=8, nbatch=2):
    B, S, W = r.shape
    tc = min(tc, S)
    npair = min(npair, W // LANES)
    nbatch = nbatch if B % nbatch == 0 else 1
    assert S % tc == 0 and tc % CHUNK == 0 and W % (npair * LANES) == 0, (S, tc, W, npair)
    C = CHUNK
    tri = jnp.tril(jnp.ones((C, C), F32)).astype(BF16)
    wl = npair * LANES
    seq = pl.BlockSpec((nbatch, tc, wl), lambda bb, p, t: (bb, t, p))
    vec = pl.BlockSpec((1, wl), lambda bb, p, t: (0, p))
    return pl.pallas_call(
        functools.partial(_rwkv_scan_kernel, nchunk=tc // C, npair=npair, nbatch=nbatch),
        grid=(B // nbatch, W // wl, S // tc),
        in_specs=[seq] * 8 + [vec, vec, pl.BlockSpec((C, C), lambda bb, p, t: (0, 0))],
        out_specs=seq,
        out_shape=jax.ShapeDtypeStruct((B, S, W), BF16),
        scratch_shapes=[pltpu.VMEM((nbatch * npair, LANES, LANES), F32)],
        compiler_params=_params(("parallel", "parallel", "arbitrary")),
        name="rwkv_scan",
    )(r, lw, k, v, kkn, b, g, bonus, gn_g, gn_b, tri)


def _mix_kernel(fo_ref, ro_ref, x_ref, wf_ref, wr_ref, fg_ref, g_ref, b_ref, o_ref, *, alpha):
    fo = fo_ref[...]
    fo = fo * lax.rsqrt(jnp.mean(fo * fo, axis=-1, keepdims=True) + FOX_EPS) * fg_ref[...]
    mix = jnp.dot(fo.astype(BF16), wf_ref[...], preferred_element_type=F32)
    mix = mix + jnp.dot(ro_ref[...].astype(BF16), wr_ref[...], preferred_element_type=F32)
    o_ref[...] = _layer_norm(alpha * x_ref[...] + mix, g_ref[...], b_ref[...])


def _mix(fo, ro, x, w_mix, layer, fox_g, g, b, alpha, tm=512):
    T, D = x.shape
    WF, WR = fo.shape[1], ro.shape[1]
    tm = min(tm, T)
    assert WF == WR and T % tm == 0, (WF, WR, T, tm)
    row = lambda i: (i, 0)
    vec = lambda i: (0, 0)
    return pl.pallas_call(
        functools.partial(_mix_kernel, alpha=alpha),
        grid=(T // tm,),
        in_specs=[pl.BlockSpec((tm, WF), row), pl.BlockSpec((tm, WR), row), pl.BlockSpec((tm, D), row),
                  pl.BlockSpec((None, WF, D), lambda i: (layer, 0, 0)),
                  pl.BlockSpec((None, WR, D), lambda i: (layer, 1, 0)),
                  pl.BlockSpec((1, WF), vec), pl.BlockSpec((1, D), vec), pl.BlockSpec((1, D), vec)],
        out_specs=pl.BlockSpec((tm, D), row),
        out_shape=jax.ShapeDtypeStruct((T, D), F32),
        compiler_params=_params(("parallel",)),
        name="mix_out",
    )(fo, ro, x, w_mix, w_mix, fox_g, g, b)


def _xattn_kernel(x_ref, wq_ref, kv_k_ref, kv_v_ref, wo_ref, g_ref, b_ref, o_ref, *, alpha, heads):
    x = x_ref[0]
    D = x.shape[1]
    dh = D // heads
    q = jnp.dot(x.astype(BF16), wq_ref[...], preferred_element_type=F32).astype(BF16)
    outs = []
    for h in range(heads):
        cols = slice(h * dh, (h + 1) * dh)
        s = _dot_nt(q[:, cols], kv_k_ref[0, :, cols]) * (dh ** -0.5)
        s = s - jnp.max(s, axis=1, keepdims=True)
        e = jnp.exp(s)
        p = e / jnp.sum(e, axis=1, keepdims=True)
        outs.append(jnp.dot(p.astype(BF16), kv_v_ref[0, :, cols], preferred_element_type=F32).astype(BF16))
    o = jnp.concatenate(outs, axis=1)
    xa = jnp.dot(o, wo_ref[...], preferred_element_type=F32)
    o_ref[0] = _layer_norm(alpha * x + xa, g_ref[...], b_ref[...])


def _xattn(x, kv, wq, wo, layer, g, b, alpha, tm=512):
    B, S, D = x.shape
    M = kv.shape[1]
    tm = min(tm, S)
    vec = lambda bb, i: (0, 0)
    return pl.pallas_call(
        functools.partial(_xattn_kernel, alpha=alpha, heads=XA_HEADS),
        grid=(B, S // tm),
        in_specs=[pl.BlockSpec((1, tm, D), lambda bb, i: (bb, i, 0)),
                  pl.BlockSpec((None, D, D), lambda bb, i: (layer, 0, 0), pipeline_mode=pl.Buffered(1)),
                  pl.BlockSpec((1, M, D), lambda bb, i: (bb, 0, 0)),
                  pl.BlockSpec((1, M, D), lambda bb, i: (bb, 0, 1)),
                  pl.BlockSpec((None, D, D), lambda bb, i: (layer, 0, 0), pipeline_mode=pl.Buffered(1)),
                  pl.BlockSpec((1, D), vec), pl.BlockSpec((1, D), vec)],
        out_specs=pl.BlockSpec((1, tm, D), lambda bb, i: (bb, i, 0)),
        out_shape=jax.ShapeDtypeStruct((B, S, D), F32),
        compiler_params=_params(("parallel", "arbitrary")),
        name="xattn",
    )(x, wq, kv, kv, wo, g, b)


def _pad_cols(w, n):
    return jnp.pad(w, ((0, 0), (0, n - w.shape[1])))


def _pad_rows(w, n):
    return jnp.pad(w, ((0, n - w.shape[0]), (0, 0)))


def kernel(x, mem, ln_g, ln_b, ffn1_w13, ffn1_w2, w_in, fox_f_bias, fox_out_g, rwkv_mu, rwkv_w0, rwkv_w_up, rwkv_a0, rwkv_a_up, rwkv_g_up, rwkv_k_k, rwkv_k_a, rwkv_r_k, rwkv_gn_g, rwkv_gn_b, w_mix_out, xa_wq, xa_wkv, xa_wo, ffn2_w13, ffn2_w2):
    B, S, D = x.shape
    depth = ln_g.shape[0]
    M = mem.shape[1]
    alpha = (2.0 * depth) ** 0.25
    fox_heads = fox_f_bias.shape[1]
    WF = fox_heads * HEAD_DIM
    WR = rwkv_w0.shape[1]
    decay_lora, aaa_lora, gate_lora = rwkv_w_up.shape[1], rwkv_a_up.shape[1], rwkv_g_up.shape[1]
    c1 = 3 * WF
    c2 = c1 + fox_heads
    T = B * S
    row = lambda t: t.reshape(1, -1)

    h = x.reshape(T, D)
    mem_b = mem.reshape(B * M, D).astype(BF16)
    ffn1_w13, ffn1_w2, ffn2_w13, ffn2_w2, w_mix_out, xa_wq, xa_wkv, xa_wo = (
        w.astype(BF16) for w in (ffn1_w13, ffn1_w2, ffn2_w13, ffn2_w2, w_mix_out, xa_wq, xa_wkv, xa_wo))
    for l in range(depth):
        h, hb = _ffn(h, ffn1_w13, ffn1_w2, l, row(ln_g[l, 0]), row(ln_b[l, 0]), alpha)

        wl = w_in[l].astype(BF16)
        w_fox = wl[:, :c1]
        w_gate = _pad_cols(wl[:, c1:c2], LANES)
        o = c2 + 3 * WR
        w_rw = jnp.concatenate([wl[:, c2:o],
                                _pad_cols(wl[:, o:o + decay_lora], LANES),
                                _pad_cols(wl[:, o + decay_lora:o + decay_lora + aaa_lora], LANES),
                                _pad_cols(wl[:, o + decay_lora + aaa_lora:], 2 * LANES)], axis=1)
        mu = rwkv_mu[l]
        o = 3 * WR
        mu_p = jnp.concatenate([mu[:o],
                                jnp.pad(mu[o:o + decay_lora], (0, LANES - decay_lora)),
                                jnp.pad(mu[o + decay_lora:o + decay_lora + aaa_lora], (0, LANES - aaa_lora)),
                                jnp.pad(mu[o + decay_lora + aaa_lora:], (0, 2 * LANES - gate_lora))])

        qkv = _matmul(hb, w_fox, BF16).reshape(B, S, c1)
        u = _matmul(hb, w_rw, F32).reshape(B, S, -1)
        bias = jnp.pad(fox_f_bias[l], (0, LANES - fox_heads)).reshape(1, LANES)
        fo = _fox_attention(qkv, _gate_cumsum(hb.reshape(B, S, D), w_gate, bias))

        r, lw, k2, v, kkn, bb, g, bonus = _rwkv_prep(
            u, row(mu_p), row(rwkv_w0[l]), row(rwkv_a0[l]), row(rwkv_k_k[l]), row(rwkv_k_a[l]),
            row(rwkv_r_k[l]),
            _pad_rows(rwkv_w_up[l], LANES).astype(BF16), _pad_rows(rwkv_a_up[l], LANES).astype(BF16),
            _pad_rows(rwkv_g_up[l], 2 * LANES).astype(BF16))
        ro = _rwkv_scan(r, lw, k2, v, kkn, bb, g, bonus, row(rwkv_gn_g[l]), row(rwkv_gn_b[l]))

        h = _mix(fo.reshape(T, WF), ro.reshape(T, WR), h, w_mix_out, l, row(fox_out_g[l]),
                 row(ln_g[l, 1]), row(ln_b[l, 1]), alpha)

        kv = _matmul(mem_b, xa_wkv, BF16, layer=l).reshape(B, M, 2 * D)
        h = _xattn(h.reshape(B, S, D), kv, xa_wq, xa_wo, l,
                   row(ln_g[l, 2]), row(ln_b[l, 2]), alpha).reshape(T, D)

        h, _ = _ffn(h, ffn2_w13, ffn2_w2, l, row(ln_g[l, 3]), row(ln_b[l, 3]), alpha)
    return h.reshape(B, S, D)
```

```python
import functools

import jax
import jax.numpy as jnp
from jax import lax
from jax.experimental import pallas as pl
from jax.experimental.pallas import tpu as pltpu

HEAD_DIM = 64
XA_HEADS = 4
LN_EPS = 1e-5
GN_EPS = 64e-5
FOX_EPS = 1e-6
NEG_INF = -1e30
LOG2_E = 1.4426950408889634
LANES = 128
CHUNK = 64
FOX_BEATS_PER_TRIP = 4
VMEM_LIMIT = 56 * 1024 * 1024

F32 = jnp.float32
BF16 = jnp.bfloat16


def _params(semantics):
    return pltpu.CompilerParams(dimension_semantics=semantics, vmem_limit_bytes=VMEM_LIMIT)


def _layer_norm(z, g, b):
    mu = jnp.mean(z, axis=-1, keepdims=True)
    zc = z - mu
    var = jnp.mean(zc * zc, axis=-1, keepdims=True)
    return zc * lax.rsqrt(var + LN_EPS) * g + b


def _split_dot(a, b01, terms):
    acc = None
    rem = a
    for _ in range(terms):
        part = rem.astype(BF16)
        d = jnp.dot(part, b01, preferred_element_type=F32)
        acc = d if acc is None else acc + d
        rem = rem - part.astype(F32)
    return acc


def _split_dot_rhs(a01, b, terms):
    acc = None
    rem = b
    for _ in range(terms):
        part = rem.astype(BF16)
        d = jnp.dot(a01, part, preferred_element_type=F32)
        acc = d if acc is None else acc + d
        rem = rem - part.astype(F32)
    return acc


def _dot_nt(a, b):
    return lax.dot_general(a, b, (((1,), (1,)), ((), ())), preferred_element_type=F32)


def _dot_tn(a, b):
    return lax.dot_general(a, b, (((0,), (0,)), ((), ())), preferred_element_type=F32)


def _ffn_kernel(x_ref, wg_ref, wu_ref, w2_ref, g_ref, b_ref, o_ref, ob_ref, acc_ref, xb_ref, *, alpha):
    j = pl.program_id(1)

    @pl.when(j == 0)
    def _():
        xb_ref[...] = x_ref[...].astype(BF16)
        acc_ref[...] = jnp.zeros_like(acc_ref)

    xb = xb_ref[...]
    gate = jnp.dot(xb, wg_ref[...], preferred_element_type=F32)
    up = jnp.dot(xb, wu_ref[...], preferred_element_type=F32)
    h = (gate * jax.nn.sigmoid(gate) * up).astype(BF16)
    acc_ref[...] += jnp.dot(h, w2_ref[...], preferred_element_type=F32)

    @pl.when(j == pl.num_programs(1) - 1)
    def _():
        y = _layer_norm(alpha * x_ref[...] + 0.5 * acc_ref[...], g_ref[...], b_ref[...])
        o_ref[...] = y
        ob_ref[...] = y.astype(BF16)


def _ffn(x, w13, w2, layer, g, b, alpha, tm=512, tf=512):
    T, D = x.shape
    F = w2.shape[1]
    tm, tf = min(tm, T), min(tf, F)
    assert T % tm == 0 and F % tf == 0, (T, tm, F, tf)
    nf = F // tf
    row = lambda i, j: (i, 0)
    vec = lambda i, j: (0, 0)
    return pl.pallas_call(
        functools.partial(_ffn_kernel, alpha=alpha),
        grid=(T // tm, nf),
        in_specs=[pl.BlockSpec((tm, D), row),
                  pl.BlockSpec((None, D, tf), lambda i, j: (layer, 0, j)),
                  pl.BlockSpec((None, D, tf), lambda i, j: (layer, 0, j + nf)),
                  pl.BlockSpec((None, tf, D), lambda i, j: (layer, j, 0)),
                  pl.BlockSpec((1, D), vec),
                  pl.BlockSpec((1, D), vec)],
        out_specs=[pl.BlockSpec((tm, D), row), pl.BlockSpec((tm, D), row)],
        out_shape=[jax.ShapeDtypeStruct((T, D), F32), jax.ShapeDtypeStruct((T, D), BF16)],
        scratch_shapes=[pltpu.VMEM((tm, D), F32), pltpu.VMEM((tm, D), BF16)],
        compiler_params=_params(("parallel", "arbitrary")),
        name="ffn",
    )(x, w13, w13, w2, g, b)


def _mm_kernel(x_ref, w_ref, o_ref):
    o_ref[...] = jnp.dot(x_ref[...], w_ref[...], preferred_element_type=F32).astype(o_ref.dtype)


def _matmul(x, w, out_dtype, layer=None, tm=1024, tn_max=1792):
    T, K = x.shape
    N = w.shape[-1]
    tm = min(tm, T)
    assert T % tm == 0, (T, tm)
    tn = max(c for c in range(LANES, min(tn_max, N) + 1, LANES) if N % c == 0)
    return pl.pallas_call(
        _mm_kernel,
        grid=(T // tm, N // tn),
        in_specs=[pl.BlockSpec((tm, K), lambda i, j: (i, 0)),
                  pl.BlockSpec((K, tn), lambda i, j: (0, j)) if layer is None else
                  pl.BlockSpec((None, K, tn), lambda i, j: (layer, 0, j))],
        out_specs=pl.BlockSpec((tm, tn), lambda i, j: (i, j)),
        out_shape=jax.ShapeDtypeStruct((T, N), out_dtype),
        compiler_params=_params(("parallel", "arbitrary")),
        name="proj",
    )(x, w)


def _gate_kernel(x_ref, w_ref, bias_ref, tri_ref, c_ref, carry_ref):
    @pl.when(pl.program_id(1) == 0)
    def _():
        carry_ref[...] = jnp.zeros_like(carry_ref)

    z = jnp.dot(x_ref[0], w_ref[...], preferred_element_type=F32) + bias_ref[...]
    logf = jnp.minimum(z, 0.0) - jnp.log(1.0 + jnp.exp(-jnp.abs(z)))
    c = _split_dot_rhs(tri_ref[...], logf, 3) + carry_ref[...]
    c_ref[0] = c
    ts = c.shape[0]
    carry_ref[...] = c[ts - 1:ts, :]


def _gate_cumsum(xb, w_gate, bias, ts=512):
    B, S, D = xb.shape
    ts = min(ts, S)
    assert S % ts == 0, (S, ts)
    tri = jnp.tril(jnp.ones((ts, ts), F32)).astype(BF16)
    return pl.pallas_call(
        _gate_kernel,
        grid=(B, S // ts),
        in_specs=[pl.BlockSpec((1, ts, D), lambda b, s: (b, s, 0)),
                  pl.BlockSpec((D, LANES), lambda b, s: (0, 0)),
                  pl.BlockSpec((1, LANES), lambda b, s: (0, 0)),
                  pl.BlockSpec((ts, ts), lambda b, s: (0, 0))],
        out_specs=pl.BlockSpec((1, ts, LANES), lambda b, s: (b, s, 0)),
        out_shape=jax.ShapeDtypeStruct((B, S, LANES), F32),
        scratch_shapes=[pltpu.VMEM((1, LANES), F32)],
        compiler_params=_params(("parallel", "arbitrary")),
        name="fox_gate",
    )(xb, w_gate, bias, tri)


def _fox_kernel(qblk_tab, tile_tab, thr_tab, q_ref, k_ref, v_ref, cc_ref, o_ref,
                kp_ref, qp_ref, vt_ref, s_scr, p_scr, acc_scr, m_scr, *, tq, n_plain, n_masked, prep_rows, scale):
    p = pl.program_id(1)
    S = k_ref.shape[1]

    def build_operands():
        rows = lax.broadcasted_iota(jnp.int32, (LANES, LANES), 0)
        cols = lax.broadcasted_iota(jnp.int32, (LANES, LANES), 1)
        ident = (rows == cols).astype(BF16)
        lane = lax.broadcasted_iota(jnp.int32, (prep_rows, LANES), 1)

        def prep(n, carry):
            sl = pl.ds(pl.multiple_of(n * prep_rows, prep_rows), prep_rows)
            kk = k_ref[0, sl, :]
            vv = v_ref[0, sl, :]
            cc = cc_ref[0, sl, :] * LOG2_E
            c1 = cc.astype(BF16)
            rem = cc - c1.astype(F32)
            c2 = rem.astype(BF16)
            c3 = (rem - c2.astype(F32)).astype(BF16)
            placed = None
            for t, part in enumerate((c1, c2, c3)):
                sel = jnp.zeros((LANES, LANES), F32)
                for e in range(2):
                    base = HEAD_DIM * (1 - e)
                    mine = rows == 2 * p + e
                    sel = sel + (mine & (cols == base + t)).astype(F32) - (mine & (cols == base + 3 + t)).astype(F32)
                d = jnp.dot(part, sel.astype(BF16), preferred_element_type=F32)
                placed = d if placed is None else placed + d
            pos = lane & (HEAD_DIM - 1)
            first3 = pos < 3
            next3 = (pos >= 3) & (pos < 6)
            k_bias = jnp.where(first3, 1.0, jnp.where(next3, placed, 0.0)).astype(BF16)
            q_bias = jnp.where(first3, placed, jnp.where(next3, 1.0, 0.0)).astype(BF16)
            qq = q_ref[0, sl, :] * (scale * LOG2_E)
            for e in range(2):
                in_head = (lane >= HEAD_DIM * e) & (lane < HEAD_DIM * (e + 1))
                kp_ref[e, sl, :] = jnp.where(in_head, kk, k_bias)
                qp_ref[e, sl, :] = jnp.where(in_head, qq, q_bias)
                v_aug = jnp.where(in_head, vv, jnp.ones_like(vv))
                vt_ref[e, :, sl] = _dot_nt(ident, v_aug).astype(BF16)
            return carry

        lax.fori_loop(0, S // prep_rows, prep, 0)

    build_operands()

    key_minus_qry = (lax.broadcasted_iota(jnp.int32, (tq, tq), 0) - lax.broadcasted_iota(jnp.int32, (tq, tq), 1))
    head_row = lax.broadcasted_iota(jnp.int32, (LANES, tq), 0)
    heads = range(2)

    def rows_of(block):
        return pl.ds(pl.multiple_of(block * tq, tq), tq)

    def scores(n, slot, masked):
        s = [_dot_nt(kp_ref[e, rows_of(tile_tab[n]), :], qp_ref[e, rows_of(qblk_tab[n]), :]) for e in heads]
        if masked:
            keep = key_minus_qry <= thr_tab[n]
            s = [jnp.where(keep, s[e], NEG_INF) for e in heads]
        for e in heads:
            s_scr[slot, e] = s[e]
        return [jnp.max(s[e], axis=0, keepdims=True) for e in heads]

    def probabilities(n, slot, tile_max):
        blk = qblk_tab[n]
        rescale = []
        for e in heads:
            m_prev = m_scr[blk, e]
            m_new = jnp.maximum(m_prev, tile_max[e])
            m_scr[blk, e] = m_new
            p_scr[slot, e] = jnp.exp2(s_scr[slot, e] - m_new).astype(BF16)
            rescale.append(jnp.exp2(m_prev - m_new))
        return rescale

    def accumulate(n, slot, rescale):
        blk = qblk_tab[n]
        for e in heads:
            acc_scr[blk, e] = rescale[e] * acc_scr[blk, e] + jnp.dot(vt_ref[e, :, rows_of(tile_tab[n])],
                                                                     p_scr[slot, e], preferred_element_type=F32)

    def run_items(first, count, masked):
        def beat(n, slot, tile_max, rescale, stages=3):
            next_max = scores(n + 2, slot, masked) if stages >= 3 else tile_max
            next_rescale = probabilities(n + 1, 1 - slot, tile_max) if stages >= 2 else rescale
            accumulate(n, slot, rescale)
            return next_max, next_rescale

        def body(jj, carry):
            tile_max, rescale = list(carry[:2]), list(carry[2:])
            for k in range(FOX_BEATS_PER_TRIP):
                tile_max, rescale = beat(first + FOX_BEATS_PER_TRIP * jj + k, k % 2, tile_max, rescale)
            return tuple(tile_max + rescale)

        rescale = probabilities(first, 0, scores(first, 0, masked))
        tile_max = scores(first + 1, 1, masked)
        trips, rest = divmod(count - 2, FOX_BEATS_PER_TRIP)
        carry = lax.fori_loop(0, trips, body, tuple(tile_max + rescale))
        tile_max, rescale = list(carry[:2]), list(carry[2:])
        done = trips * FOX_BEATS_PER_TRIP
        for k in range(rest):
            tile_max, rescale = beat(first + done + k, k % 2, tile_max, rescale)
        tile_max, rescale = beat(first + count - 2, 0, tile_max, rescale, stages=2)
        beat(first + count - 1, 1, tile_max, rescale, stages=1)

    acc_scr[...] = jnp.zeros_like(acc_scr)
    m_scr[...] = jnp.full_like(m_scr, NEG_INF)
    if n_plain:
        run_items(0, n_plain, False)
    run_items(n_plain, n_masked, True)

    def finish(blk, carry):
        out = [acc_scr[blk, e] / acc_scr[blk, e, HEAD_DIM * (1 - e):HEAD_DIM * (1 - e) + 1, :] for e in heads]
        o_ref[0, rows_of(blk), :] = jnp.where(head_row < HEAD_DIM, out[0], out[1]).T
        return carry

    lax.fori_loop(0, S // tq, finish, 0)


def _fox_attention(qkv, c, tq=512):
    B, S, W3 = qkv.shape
    W = W3 // 3
    npair = W // LANES
    tq = min(tq, S)
    assert S % tq == 0 and W % LANES == 0, (S, tq, W)
    nq = S // tq
    below = [(i, t, 2 * tq) for i in range(nq) for t in range(i)]
    n_plain = len(below) - len(below) % 2
    masked = below[n_plain:] + [(i, i, 0) for i in range(nq)]
    masked += [(nq - 1, nq - 1, -2 * tq)] * (len(masked) % 2)
    items = below[:n_plain] + masked
    tabs = [jnp.asarray([it[n] for it in items], jnp.int32) for n in range(3)]
    seq = lambda col: pl.BlockSpec((1, S, LANES), lambda b, p, *_: (b, 0, col(p)))
    return pl.pallas_call(
        functools.partial(_fox_kernel, tq=tq, n_plain=n_plain, n_masked=len(masked),
                          prep_rows=tq * (2 if nq % 2 == 0 else 1), scale=HEAD_DIM ** -0.5),
        grid_spec=pltpu.PrefetchScalarGridSpec(
            num_scalar_prefetch=3,
            grid=(B, npair),
            in_specs=[seq(lambda p: p), seq(lambda p: npair + p), seq(lambda p: 2 * npair + p), seq(lambda p: 0)],
            out_specs=seq(lambda p: p),
            scratch_shapes=[pltpu.VMEM((2, S, LANES), BF16), pltpu.VMEM((2, S, LANES), BF16),
                            pltpu.VMEM((2, LANES, S), BF16),
                            pltpu.VMEM((2, 2, tq, tq), F32), pltpu.VMEM((2, 2, tq, tq), BF16),
                            pltpu.VMEM((nq, 2, LANES, tq), F32), pltpu.VMEM((nq, 2, 1, tq), F32)]),
        out_shape=jax.ShapeDtypeStruct((B, S, W), F32),
        compiler_params=_params(("parallel", "parallel")),
        name="fox_attn",
    )(*tabs, qkv, qkv, qkv, c)


def _rwkv_prep_kernel(u_ref, up_ref, mu_ref, w0_ref, a0_ref, kk_ref, ka_ref, rk_ref, wup_ref, aup_ref, gup_ref,
                      seg_ref, segt_ref,
                      r_o, lw_o, k_o, v_o, kkn_o, b_o, g_o, bonus_o, *, tm, width):
    i = pl.program_id(1)
    first_row = lax.broadcasted_iota(jnp.int32, (tm, 1), 0) == 0
    has_prev = i > 0

    def shifted(lo, hi):
        u = u_ref[0, :, lo:hi]
        prev_last = jnp.where(has_prev, up_ref[0, 7:8, lo:hi], 0.0)
        u_prev = jnp.where(first_row, prev_last, pltpu.roll(u, 1, 0))
        return u + mu_ref[:, lo:hi] * (u_prev - u)

    def head_sum(t):
        per_head = _split_dot(t, seg_ref[...], 2)
        return _split_dot(per_head, segt_ref[...], 2)

    W = width
    r = shifted(0, W)
    k = shifted(W, 2 * W)
    v = shifted(2 * W, 3 * W)
    wd = shifted(3 * W, 3 * W + LANES)
    ad = shifted(3 * W + LANES, 3 * W + 2 * LANES)
    gd = shifted(3 * W + 2 * LANES, 3 * W + 4 * LANES)

    wlin = w0_ref[...] + jnp.dot(jnp.tanh(wd).astype(BF16), wup_ref[...], preferred_element_type=F32)
    z = -wlin
    w = -(jnp.maximum(z, 0.0) + jnp.log(1.0 + jnp.exp(-jnp.abs(z)))) - 0.5
    lw_o[0] = -jnp.exp(w)
    a = jax.nn.sigmoid(a0_ref[...] + jnp.dot(ad.astype(BF16), aup_ref[...], preferred_element_type=F32))
    g = jnp.dot(jax.nn.sigmoid(gd).astype(BF16), gup_ref[...], preferred_element_type=F32)
    g_o[0] = g.astype(g_o.dtype)
    kk = k * kk_ref[...]
    kkn = kk / jnp.maximum(jnp.sqrt(head_sum(kk * kk)), 1e-12)
    k2 = k * (1.0 + (a - 1.0) * ka_ref[...])
    r_o[0] = r.astype(r_o.dtype)
    k_o[0] = k2.astype(k_o.dtype)
    v_o[0] = v.astype(v_o.dtype)
    kkn_o[0] = kkn.astype(kkn_o.dtype)
    b_o[0] = (a * kkn).astype(b_o.dtype)
    bonus_o[0] = (head_sum(r * k2 * rk_ref[...]) * v).astype(bonus_o.dtype)


def _rwkv_prep(u, mu, w0, a0, k_k, k_a, r_k, w_up, a_up, g_up, tm=256):
    B, S, UW = u.shape
    W = w0.shape[1]
    tm = min(tm, S)
    nh = W // HEAD_DIM
    head_of = jnp.arange(W) // HEAD_DIM
    seg = (head_of[:, None] == jnp.arange(LANES)[None, :]).astype(BF16)
    segt = seg.T
    del nh
    blk = lambda b, i: (b, i, 0)
    vec = lambda b, i: (0, 0)
    out = [jax.ShapeDtypeStruct((B, S, W), F32 if n == 1 else BF16) for n in range(8)]
    return pl.pallas_call(
        functools.partial(_rwkv_prep_kernel, tm=tm, width=W),
        grid=(B, S // tm),
        in_specs=[pl.BlockSpec((1, tm, UW), blk),
                  pl.BlockSpec((1, 8, UW), lambda b, i: (b, jnp.maximum(i * (tm // 8) - 1, 0), 0)),
                  pl.BlockSpec((1, UW), vec),
                  pl.BlockSpec((1, W), vec), pl.BlockSpec((1, W), vec), pl.BlockSpec((1, W), vec),
                  pl.BlockSpec((1, W), vec), pl.BlockSpec((1, W), vec),
                  pl.BlockSpec((LANES, W), vec), pl.BlockSpec((LANES, W), vec), pl.BlockSpec((2 * LANES, W), vec),
                  pl.BlockSpec((W, LANES), vec), pl.BlockSpec((LANES, W), vec)],
        out_specs=[pl.BlockSpec((1, tm, W), blk)] * 8,
        out_shape=out,
        compiler_params=_params(("parallel", "arbitrary")),
        name="rwkv_prep",
    )(u, u, mu, w0, a0, k_k, k_a, r_k, w_up, a_up, g_up, seg, segt)


def _rwkv_scan_kernel(r_ref, lw_ref, k_ref, v_ref, kkn_ref, b_ref, g_ref, bonus_ref, gng_ref, gnb_ref,
                      tri_ref, o_ref, s_ref, *, nchunk, npair, nbatch):
    C = CHUNK

    @pl.when(pl.program_id(2) == 0)
    def _():
        s_ref[...] = jnp.zeros_like(s_ref)

    t_i = lax.broadcasted_iota(jnp.int32, (C, LANES), 0)
    lane = lax.broadcasted_iota(jnp.int32, (C, LANES), 1)
    s_i = lane & (HEAD_DIM - 1)
    low = lane < HEAD_DIM
    strict = s_i < t_i
    incl = s_i <= t_i
    eye = (s_i == t_i).astype(F32)
    levels = []
    k_ = 0
    while (1 << k_) < C:
        levels.append(((t_i >> k_) == ((s_i >> k_) + 1)) & ((t_i >> (k_ + 1)) == (s_i >> (k_ + 1))))
        k_ += 1
    blk_r = lax.broadcasted_iota(jnp.int32, (LANES, LANES), 0) // HEAD_DIM
    blk_c = lax.broadcasted_iota(jnp.int32, (LANES, LANES), 1) // HEAD_DIM
    same_head = blk_r == blk_c

    def block_diag(y):
        zero = jnp.zeros_like(y)
        return jnp.concatenate([jnp.where(low, y, zero), jnp.where(low, zero, y)], axis=0)

    def head_dot(x, y):
        return jnp.dot(x.astype(BF16), block_diag(y.astype(BF16)), preferred_element_type=F32)

    def head_mean(t):
        lo = jnp.sum(jnp.where(low, t, 0.0), axis=1, keepdims=True)
        hi = jnp.sum(jnp.where(low, 0.0, t), axis=1, keepdims=True)
        return jnp.where(low, lo, hi) * (1.0 / HEAD_DIM)

    tri = tri_ref[...]

    def chunk(c, carry):
        sl = pl.ds(pl.multiple_of(c * C, C), C)
        pairs = range(nbatch * npair)
        col = lambda t, p: t[:, p * LANES:(p + 1) * LANES]
        rows = lambda ref: jnp.concatenate([ref[bb, sl, :] for bb in range(nbatch)], axis=1)
        r = rows(r_ref)
        lw = rows(lw_ref)
        k = rows(k_ref)
        v = rows(v_ref)
        kkn = rows(kkn_ref)
        b = rows(b_ref)

        cum = _split_dot_rhs(tri, lw, 2)
        cum_last = cum[C - 1:C, :]
        g_inv = jnp.exp(-cum)
        g_tail = jnp.exp(cum_last - cum)
        a_t = (-kkn * jnp.exp(cum - lw)).astype(BF16)
        r_t = (r * jnp.exp(cum)).astype(BF16)
        k_t = (k * g_inv).astype(BF16)
        b_t = (b * g_inv).astype(BF16)
        k_tail = (k * g_tail).astype(BF16)
        b_tail = (b * g_tail).astype(BF16)
        v16 = v.astype(BF16)
        decay_last = jnp.exp(cum_last)

        s0 = [s_ref[p] for p in pairs]
        lhs = [jnp.concatenate([col(a_t, p), col(r_t, p)], axis=0) for p in pairs]
        amat = [_dot_nt(lhs[p], jnp.concatenate([block_diag(col(k_t, p)), block_diag(col(b_t, p))], axis=0))
                for p in pairs]
        s_proj = [_dot_nt(lhs[p], s0[p].astype(BF16)) for p in pairs]
        n_ab = [jnp.where(strict, amat[p][:C, LANES:], 0.0) for p in pairs]
        inv = [eye + jnp.where(levels[0], n_ab[p], 0.0) for p in pairs]
        for lvl in levels[1:]:
            x = [head_dot(inv[p], jnp.where(lvl, n_ab[p], 0.0)) for p in pairs]
            inv = [inv[p] + head_dot(x[p], inv[p]) for p in pairs]

        vbd = [block_diag(col(v16, p)) for p in pairs]
        rhs_u = [s_proj[p][:C] + jnp.dot(jnp.where(strict, amat[p][:C, :LANES], 0.0).astype(BF16), vbd[p],
                                         preferred_element_type=F32) for p in pairs]
        u = [head_dot(inv[p], rhs_u[p]) for p in pairs]
        u16 = [u[p].astype(BF16) for p in pairs]
        y = [s_proj[p][C:] + jnp.dot(jnp.where(jnp.concatenate([incl, incl], axis=1), amat[p][C:], 0.0).astype(BF16),
                                     jnp.concatenate([vbd[p], block_diag(u16[p])], axis=0),
                                     preferred_element_type=F32) for p in pairs]
        s_upd = [_dot_tn(jnp.concatenate([col(v16, p), u16[p]], axis=0),
                         jnp.concatenate([col(k_tail, p), col(b_tail, p)], axis=0)) for p in pairs]
        s_ref[...] = jnp.stack([s0[p] * col(decay_last, p) + jnp.where(same_head, s_upd[p], 0.0) for p in pairs],
                               axis=0)

        yc = [y[p] - head_mean(y[p]) for p in pairs]
        var = [head_mean(yc[p] * yc[p]) for p in pairs]
        yn = jnp.concatenate([yc[p] * lax.rsqrt(var[p] + GN_EPS) for p in pairs], axis=1)
        gn_g = jnp.concatenate([gng_ref[...]] * nbatch, axis=1)
        gn_b = jnp.concatenate([gnb_ref[...]] * nbatch, axis=1)
        out = ((yn * gn_g + gn_b + rows(bonus_ref)) * rows(g_ref)).astype(o_ref.dtype)
        width = npair * LANES
        for bb in range(nbatch):
            o_ref[bb, sl, :] = out[:, bb * width:(bb + 1) * width]
        return carry

    lax.fori_loop(0, nchunk, chunk, 0)


def _rwkv_scan(r, lw, k, v, kkn, b, g, bonus, gn_g, gn_b, tc=256, npair=8, nbatch=4):
    B, S, W = r.shape
    tc = min(tc, S)
    npair = min(npair, W // LANES)
    nbatch = nbatch if B % nbatch == 0 else 1
    assert S % tc == 0 and tc % CHUNK == 0 and W % (npair * LANES) == 0, (S, tc, W, npair)
    C = CHUNK
    tri = jnp.tril(jnp.ones((C, C), F32)).astype(BF16)
    wl = npair * LANES
    seq = pl.BlockSpec((nbatch, tc, wl), lambda bb, p, t: (bb, t, p))
    vec = pl.BlockSpec((1, wl), lambda bb, p, t: (0, p))
    return pl.pallas_call(
        functools.partial(_rwkv_scan_kernel, nchunk=tc // C, npair=npair, nbatch=nbatch),
        grid=(B // nbatch, W // wl, S // tc),
        in_specs=[seq] * 8 + [vec, vec, pl.BlockSpec((C, C), lambda bb, p, t: (0, 0))],
        out_specs=seq,
        out_shape=jax.ShapeDtypeStruct((B, S, W), BF16),
        scratch_shapes=[pltpu.VMEM((nbatch * npair, LANES, LANES), F32)],
        compiler_params=_params(("parallel", "parallel", "arbitrary")),
        name="rwkv_scan",
    )(r, lw, k, v, kkn, b, g, bonus, gn_g, gn_b, tri)


def _mix_kernel(fo_ref, ro_ref, x_ref, wf_ref, wr_ref, fg_ref, g_ref, b_ref, o_ref, *, alpha):
    fo = fo_ref[...]
    fo = fo * lax.rsqrt(jnp.mean(fo * fo, axis=-1, keepdims=True) + FOX_EPS) * fg_ref[...]
    mix = jnp.dot(fo.astype(BF16), wf_ref[...], preferred_element_type=F32)
    mix = mix + jnp.dot(ro_ref[...].astype(BF16), wr_ref[...], preferred_element_type=F32)
    o_ref[...] = _layer_norm(alpha * x_ref[...] + mix, g_ref[...], b_ref[...])


def _mix(fo, ro, x, w_mix, layer, fox_g, g, b, alpha, tm=512):
    T, D = x.shape
    WF, WR = fo.shape[1], ro.shape[1]
    tm = min(tm, T)
    assert WF == WR and T % tm == 0, (WF, WR, T, tm)
    row = lambda i: (i, 0)
    vec = lambda i: (0, 0)
    return pl.pallas_call(
        functools.partial(_mix_kernel, alpha=alpha),
        grid=(T // tm,),
        in_specs=[pl.BlockSpec((tm, WF), row), pl.BlockSpec((tm, WR), row), pl.BlockSpec((tm, D), row),
                  pl.BlockSpec((None, WF, D), lambda i: (layer, 0, 0)),
                  pl.BlockSpec((None, WR, D), lambda i: (layer, 1, 0)),
                  pl.BlockSpec((1, WF), vec), pl.BlockSpec((1, D), vec), pl.BlockSpec((1, D), vec)],
        out_specs=pl.BlockSpec((tm, D), row),
        out_shape=jax.ShapeDtypeStruct((T, D), F32),
        compiler_params=_params(("parallel",)),
        name="mix_out",
    )(fo, ro, x, w_mix, w_mix, fox_g, g, b)


def _xattn_kernel(x_ref, wq_ref, kv_k_ref, kv_v_ref, wo_ref, g_ref, b_ref, o_ref, *, alpha, heads):
    x = x_ref[0]
    D = x.shape[1]
    dh = D // heads
    q = jnp.dot(x.astype(BF16), wq_ref[...], preferred_element_type=F32).astype(BF16)
    outs = []
    for h in range(heads):
        cols = slice(h * dh, (h + 1) * dh)
        s = _dot_nt(q[:, cols], kv_k_ref[0, :, cols]) * (dh ** -0.5)
        s = s - jnp.max(s, axis=1, keepdims=True)
        e = jnp.exp(s)
        p = e / jnp.sum(e, axis=1, keepdims=True)
        outs.append(jnp.dot(p.astype(BF16), kv_v_ref[0, :, cols], preferred_element_type=F32).astype(BF16))
    o = jnp.concatenate(outs, axis=1)
    xa = jnp.dot(o, wo_ref[...], preferred_element_type=F32)
    o_ref[0] = _layer_norm(alpha * x + xa, g_ref[...], b_ref[...])


def _xattn(x, kv, wq, wo, layer, g, b, alpha, tm=512):
    B, S, D = x.shape
    M = kv.shape[1]
    tm = min(tm, S)
    vec = lambda bb, i: (0, 0)
    return pl.pallas_call(
        functools.partial(_xattn_kernel, alpha=alpha, heads=XA_HEADS),
        grid=(B, S // tm),
        in_specs=[pl.BlockSpec((1, tm, D), lambda bb, i: (bb, i, 0)),
                  pl.BlockSpec((None, D, D), lambda bb, i: (layer, 0, 0), pipeline_mode=pl.Buffered(1)),
                  pl.BlockSpec((1, M, D), lambda bb, i: (bb, 0, 0)),
                  pl.BlockSpec((1, M, D), lambda bb, i: (bb, 0, 1)),
                  pl.BlockSpec((None, D, D), lambda bb, i: (layer, 0, 0), pipeline_mode=pl.Buffered(1)),
                  pl.BlockSpec((1, D), vec), pl.BlockSpec((1, D), vec)],
        out_specs=pl.BlockSpec((1, tm, D), lambda bb, i: (bb, i, 0)),
        out_shape=jax.ShapeDtypeStruct((B, S, D), F32),
        compiler_params=_params(("parallel", "arbitrary")),
        name="xattn",
    )(x, wq, kv, kv, wo, g, b)


def _pad_cols(w, n):
    return jnp.pad(w, ((0, 0), (0, n - w.shape[1])))


def _pad_rows(w, n):
    return jnp.pad(w, ((0, n - w.shape[0]), (0, 0)))


def kernel(x, mem, ln_g, ln_b, ffn1_w13, ffn1_w2, w_in, fox_f_bias, fox_out_g, rwkv_mu, rwkv_w0, rwkv_w_up, rwkv_a0, rwkv_a_up, rwkv_g_up, rwkv_k_k, rwkv_k_a, rwkv_r_k, rwkv_gn_g, rwkv_gn_b, w_mix_out, xa_wq, xa_wkv, xa_wo, ffn2_w13, ffn2_w2):
    B, S, D = x.shape
    depth = ln_g.shape[0]
    M = mem.shape[1]
    alpha = (2.0 * depth) ** 0.25
    fox_heads = fox_f_bias.shape[1]
    WF = fox_heads * HEAD_DIM
    WR = rwkv_w0.shape[1]
    decay_lora, aaa_lora, gate_lora = rwkv_w_up.shape[1], rwkv_a_up.shape[1], rwkv_g_up.shape[1]
    c1 = 3 * WF
    c2 = c1 + fox_heads
    T = B * S
    row = lambda t: t.reshape(1, -1)

    h = x.reshape(T, D)
    mem_b = mem.reshape(B * M, D).astype(BF16)
    ffn1_w13, ffn1_w2, ffn2_w13, ffn2_w2, w_mix_out, xa_wq, xa_wkv, xa_wo = (
        w.astype(BF16) for w in (ffn1_w13, ffn1_w2, ffn2_w13, ffn2_w2, w_mix_out, xa_wq, xa_wkv, xa_wo))
    for l in range(depth):
        h, hb = _ffn(h, ffn1_w13, ffn1_w2, l, row(ln_g[l, 0]), row(ln_b[l, 0]), alpha)

        wl = w_in[l].astype(BF16)
        w_fox = wl[:, :c1]
        w_gate = _pad_cols(wl[:, c1:c2], LANES)
        o = c2 + 3 * WR
        w_rw = jnp.concatenate([wl[:, c2:o],
                                _pad_cols(wl[:, o:o + decay_lora], LANES),
                                _pad_cols(wl[:, o + decay_lora:o + decay_lora + aaa_lora], LANES),
                                _pad_cols(wl[:, o + decay_lora + aaa_lora:], 2 * LANES)], axis=1)
        mu = rwkv_mu[l]
        o = 3 * WR
        mu_p = jnp.concatenate([mu[:o],
                                jnp.pad(mu[o:o + decay_lora], (0, LANES - decay_lora)),
                                jnp.pad(mu[o + decay_lora:o + decay_lora + aaa_lora], (0, LANES - aaa_lora)),
                                jnp.pad(mu[o + decay_lora + aaa_lora:], (0, 2 * LANES - gate_lora))])

        qkv = _matmul(hb, w_fox, BF16).reshape(B, S, c1)
        u = _matmul(hb, w_rw, F32).reshape(B, S, -1)
        bias = jnp.pad(fox_f_bias[l], (0, LANES - fox_heads)).reshape(1, LANES)
        fo = _fox_attention(qkv, _gate_cumsum(hb.reshape(B, S, D), w_gate, bias))

        r, lw, k2, v, kkn, bb, g, bonus = _rwkv_prep(
            u, row(mu_p), row(rwkv_w0[l]), row(rwkv_a0[l]), row(rwkv_k_k[l]), row(rwkv_k_a[l]),
            row(rwkv_r_k[l]),
            _pad_rows(rwkv_w_up[l], LANES).astype(BF16), _pad_rows(rwkv_a_up[l], LANES).astype(BF16),
            _pad_rows(rwkv_g_up[l], 2 * LANES).astype(BF16))
        ro = _rwkv_scan(r, lw, k2, v, kkn, bb, g, bonus, row(rwkv_gn_g[l]), row(rwkv_gn_b[l]))

        h = _mix(fo.reshape(T, WF), ro.reshape(T, WR), h, w_mix_out, l, row(fox_out_g[l]),
                 row(ln_g[l, 1]), row(ln_b[l, 1]), alpha)

        kv = _matmul(mem_b, xa_wkv, BF16, layer=l).reshape(B, M, 2 * D)
        h = _xattn(h.reshape(B, S, D), kv, xa_wq, xa_wo, l,
                   row(ln_g[l, 2]), row(ln_b[l, 2]), alpha).reshape(T, D)

        h, _ = _ffn(h, ffn2_w13, ffn2_w2, l, row(ln_g[l, 3]), row(ln_b[l, 3]), alpha)
    return h.reshape(B, S, D)
```

```python
import functools

import jax
import jax.numpy as jnp
from jax import lax
from jax.experimental import pallas as pl
from jax.experimental.pallas import tpu as pltpu

HEAD_DIM = 64
XA_HEADS = 4
LN_EPS = 1e-5
GN_EPS = 64e-5
FOX_EPS = 1e-6
NEG_INF = -1e30
LOG2_E = 1.4426950408889634
LANES = 128
CHUNK = 64
FOX_BEATS_PER_TRIP = 4
VMEM_LIMIT = 56 * 1024 * 1024

F32 = jnp.float32
BF16 = jnp.bfloat16


def _params(semantics):
    return pltpu.CompilerParams(dimension_semantics=semantics, vmem_limit_bytes=VMEM_LIMIT)


def _layer_norm(z, g, b):
    mu = jnp.mean(z, axis=-1, keepdims=True)
    zc = z - mu
    var = jnp.mean(zc * zc, axis=-1, keepdims=True)
    return zc * lax.rsqrt(var + LN_EPS) * g + b


def _split_dot(a, b01, terms):
    acc = None
    rem = a
    for _ in range(terms):
        part = rem.astype(BF16)
        d = jnp.dot(part, b01, preferred_element_type=F32)
        acc = d if acc is None else acc + d
        rem = rem - part.astype(F32)
    return acc


def _split_dot_rhs(a01, b, terms):
    acc = None
    rem = b
    for _ in range(terms):
        part = rem.astype(BF16)
        d = jnp.dot(a01, part, preferred_element_type=F32)
        acc = d if acc is None else acc + d
        rem = rem - part.astype(F32)
    return acc


def _dot_nt(a, b):
    return lax.dot_general(a, b, (((1,), (1,)), ((), ())), preferred_element_type=F32)


def _dot_tn(a, b):
    return lax.dot_general(a, b, (((0,), (0,)), ((), ())), preferred_element_type=F32)


def _ffn_kernel(x_ref, wg_ref, wu_ref, w2_ref, g_ref, b_ref, o_ref, ob_ref, acc_ref, xb_ref, *, alpha):
    j = pl.program_id(1)

    @pl.when(j == 0)
    def _():
        xb_ref[...] = x_ref[...].astype(BF16)
        acc_ref[...] = jnp.zeros_like(acc_ref)

    xb = xb_ref[...]
    gate = jnp.dot(xb, wg_ref[...], preferred_element_type=F32)
    up = jnp.dot(xb, wu_ref[...], preferred_element_type=F32)
    h = (gate * jax.nn.sigmoid(gate) * up).astype(BF16)
    acc_ref[...] += jnp.dot(h, w2_ref[...], preferred_element_type=F32)

    @pl.when(j == pl.num_programs(1) - 1)
    def _():
        y = _layer_norm(alpha * x_ref[...] + 0.5 * acc_ref[...], g_ref[...], b_ref[...])
        o_ref[...] = y
        ob_ref[...] = y.astype(BF16)


def _ffn(x, w13, w2, layer, g, b, alpha, tm=512, tf=512):
    T, D = x.shape
    F = w2.shape[1]
    tm, tf = min(tm, T), min(tf, F)
    assert T % tm == 0 and F % tf == 0, (T, tm, F, tf)
    nf = F // tf
    row = lambda i, j: (i, 0)
    vec = lambda i, j: (0, 0)
    return pl.pallas_call(
        functools.partial(_ffn_kernel, alpha=alpha),
        grid=(T // tm, nf),
        in_specs=[pl.BlockSpec((tm, D), row),
                  pl.BlockSpec((None, D, tf), lambda i, j: (layer, 0, j)),
                  pl.BlockSpec((None, D, tf), lambda i, j: (layer, 0, j + nf)),
                  pl.BlockSpec((None, tf, D), lambda i, j: (layer, j, 0)),
                  pl.BlockSpec((1, D), vec),
                  pl.BlockSpec((1, D), vec)],
        out_specs=[pl.BlockSpec((tm, D), row), pl.BlockSpec((tm, D), row)],
        out_shape=[jax.ShapeDtypeStruct((T, D), F32), jax.ShapeDtypeStruct((T, D), BF16)],
        scratch_shapes=[pltpu.VMEM((tm, D), F32), pltpu.VMEM((tm, D), BF16)],
        compiler_params=_params(("parallel", "arbitrary")),
        name="ffn",
    )(x, w13, w13, w2, g, b)


def _mm_kernel(x_ref, w_ref, o_ref):
    o_ref[...] = jnp.dot(x_ref[...], w_ref[...], preferred_element_type=F32).astype(o_ref.dtype)


def _matmul(x, w, out_dtype, layer=None, tm=1024, tn_max=1792):
    T, K = x.shape
    N = w.shape[-1]
    tm = min(tm, T)
    assert T % tm == 0, (T, tm)
    tn = max(c for c in range(LANES, min(tn_max, N) + 1, LANES) if N % c == 0)
    return pl.pallas_call(
        _mm_kernel,
        grid=(T // tm, N // tn),
        in_specs=[pl.BlockSpec((tm, K), lambda i, j: (i, 0)),
                  pl.BlockSpec((K, tn), lambda i, j: (0, j)) if layer is None else
                  pl.BlockSpec((None, K, tn), lambda i, j: (layer, 0, j))],
        out_specs=pl.BlockSpec((tm, tn), lambda i, j: (i, j)),
        out_shape=jax.ShapeDtypeStruct((T, N), out_dtype),
        compiler_params=_params(("parallel", "arbitrary")),
        name="proj",
    )(x, w)


def _gate_kernel(x_ref, w_ref, bias_ref, tri_ref, c_ref, carry_ref):
    @pl.when(pl.program_id(1) == 0)
    def _():
        carry_ref[...] = jnp.zeros_like(carry_ref)

    z = jnp.dot(x_ref[0], w_ref[...], preferred_element_type=F32) + bias_ref[...]
    logf = jnp.minimum(z, 0.0) - jnp.log(1.0 + jnp.exp(-jnp.abs(z)))
    c = _split_dot_rhs(tri_ref[...], logf, 3) + carry_ref[...]
    c_ref[0] = c
    ts = c.shape[0]
    carry_ref[...] = c[ts - 1:ts, :]


def _gate_cumsum(xb, w_gate, bias, ts=512):
    B, S, D = xb.shape
    ts = min(ts, S)
    assert S % ts == 0, (S, ts)
    tri = jnp.tril(jnp.ones((ts, ts), F32)).astype(BF16)
    return pl.pallas_call(
        _gate_kernel,
        grid=(B, S // ts),
        in_specs=[pl.BlockSpec((1, ts, D), lambda b, s: (b, s, 0)),
                  pl.BlockSpec((D, LANES), lambda b, s: (0, 0)),
                  pl.BlockSpec((1, LANES), lambda b, s: (0, 0)),
                  pl.BlockSpec((ts, ts), lambda b, s: (0, 0))],
        out_specs=pl.BlockSpec((1, ts, LANES), lambda b, s: (b, s, 0)),
        out_shape=jax.ShapeDtypeStruct((B, S, LANES), F32),
        scratch_shapes=[pltpu.VMEM((1, LANES), F32)],
        compiler_params=_params(("parallel", "arbitrary")),
        name="fox_gate",
    )(xb, w_gate, bias, tri)


def _fox_kernel(qblk_tab, tile_tab, thr_tab, q_ref, k_ref, v_ref, cc_ref, o_ref,
                kp_ref, qp_ref, vt_ref, s_scr, p_scr, acc_scr, m_scr, *, tq, n_plain, n_masked, prep_rows, scale):
    p = pl.program_id(1)
    S = k_ref.shape[1]

    def build_operands():
        rows = lax.broadcasted_iota(jnp.int32, (LANES, LANES), 0)
        cols = lax.broadcasted_iota(jnp.int32, (LANES, LANES), 1)
        ident = (rows == cols).astype(BF16)
        lane = lax.broadcasted_iota(jnp.int32, (prep_rows, LANES), 1)

        def prep(n, carry):
            sl = pl.ds(pl.multiple_of(n * prep_rows, prep_rows), prep_rows)
            kk = k_ref[0, sl, :]
            vv = v_ref[0, sl, :]
            cc = cc_ref[0, sl, :] * LOG2_E
            c1 = cc.astype(BF16)
            rem = cc - c1.astype(F32)
            c2 = rem.astype(BF16)
            c3 = (rem - c2.astype(F32)).astype(BF16)
            placed = None
            for t, part in enumerate((c1, c2, c3)):
                sel = jnp.zeros((LANES, LANES), F32)
                for e in range(2):
                    base = HEAD_DIM * (1 - e)
                    mine = rows == 2 * p + e
                    sel = sel + (mine & (cols == base + t)).astype(F32) - (mine & (cols == base + 3 + t)).astype(F32)
                d = jnp.dot(part, sel.astype(BF16), preferred_element_type=F32)
                placed = d if placed is None else placed + d
            pos = lane & (HEAD_DIM - 1)
            first3 = pos < 3
            next3 = (pos >= 3) & (pos < 6)
            k_bias = jnp.where(first3, 1.0, jnp.where(next3, placed, 0.0)).astype(BF16)
            q_bias = jnp.where(first3, placed, jnp.where(next3, 1.0, 0.0)).astype(BF16)
            qq = q_ref[0, sl, :] * (scale * LOG2_E)
            for e in range(2):
                in_head = (lane >= HEAD_DIM * e) & (lane < HEAD_DIM * (e + 1))
                kp_ref[e, sl, :] = jnp.where(in_head, kk, k_bias)
                qp_ref[e, sl, :] = jnp.where(in_head, qq, q_bias)
                v_aug = jnp.where(in_head, vv, jnp.ones_like(vv))
                vt_ref[e, :, sl] = _dot_nt(ident, v_aug).astype(BF16)
            return carry

        lax.fori_loop(0, S // prep_rows, prep, 0)

    build_operands()

    key_minus_qry = (lax.broadcasted_iota(jnp.int32, (tq, tq), 0) - lax.broadcasted_iota(jnp.int32, (tq, tq), 1))
    head_row = lax.broadcasted_iota(jnp.int32, (LANES, tq), 0)
    heads = range(2)

    def rows_of(block):
        return pl.ds(pl.multiple_of(block * tq, tq), tq)

    def scores(n, slot, masked):
        s = [_dot_nt(kp_ref[e, rows_of(tile_tab[n]), :], qp_ref[e, rows_of(qblk_tab[n]), :]) for e in heads]
        if masked:
            keep = key_minus_qry <= thr_tab[n]
            s = [jnp.where(keep, s[e], NEG_INF) for e in heads]
        for e in heads:
            s_scr[slot, e] = s[e]
        return [jnp.max(s[e], axis=0, keepdims=True) for e in heads]

    def probabilities(n, slot, tile_max):
        blk = qblk_tab[n]
        rescale = []
        for e in heads:
            m_prev = m_scr[blk, e]
            m_new = jnp.maximum(m_prev, tile_max[e])
            m_scr[blk, e] = m_new
            p_scr[slot, e] = jnp.exp2(s_scr[slot, e] - m_new).astype(BF16)
            rescale.append(jnp.exp2(m_prev - m_new))
        return rescale

    def accumulate(n, slot, rescale):
        blk = qblk_tab[n]
        for e in heads:
            acc_scr[blk, e] = rescale[e] * acc_scr[blk, e] + jnp.dot(vt_ref[e, :, rows_of(tile_tab[n])],
                                                                     p_scr[slot, e], preferred_element_type=F32)

    def run_items(first, count, masked):
        def beat(n, slot, tile_max, rescale, stages=3):
            next_max = scores(n + 2, slot, masked) if stages >= 3 else tile_max
            next_rescale = probabilities(n + 1, 1 - slot, tile_max) if stages >= 2 else rescale
            accumulate(n, slot, rescale)
            return next_max, next_rescale

        def body(jj, carry):
            tile_max, rescale = list(carry[:2]), list(carry[2:])
            for k in range(FOX_BEATS_PER_TRIP):
                tile_max, rescale = beat(first + FOX_BEATS_PER_TRIP * jj + k, k % 2, tile_max, rescale)
            return tuple(tile_max + rescale)

        rescale = probabilities(first, 0, scores(first, 0, masked))
        tile_max = scores(first + 1, 1, masked)
        trips, rest = divmod(count - 2, FOX_BEATS_PER_TRIP)
        carry = lax.fori_loop(0, trips, body, tuple(tile_max + rescale))
        tile_max, rescale = list(carry[:2]), list(carry[2:])
        done = trips * FOX_BEATS_PER_TRIP
        for k in range(rest):
            tile_max, rescale = beat(first + done + k, k % 2, tile_max, rescale)
        tile_max, rescale = beat(first + count - 2, 0, tile_max, rescale, stages=2)
        beat(first + count - 1, 1, tile_max, rescale, stages=1)

    acc_scr[...] = jnp.zeros_like(acc_scr)
    m_scr[...] = jnp.full_like(m_scr, NEG_INF)
    if n_plain:
        run_items(0, n_plain, False)
    run_items(n_plain, n_masked, True)

    def finish(blk, carry):
        out = [acc_scr[blk, e] / acc_scr[blk, e, HEAD_DIM * (1 - e):HEAD_DIM * (1 - e) + 1, :] for e in heads]
        o_ref[0, rows_of(blk), :] = jnp.where(head_row < HEAD_DIM, out[0], out[1]).T
        return carry

    lax.fori_loop(0, S // tq, finish, 0)


def _fox_attention(qkv, c, tq=512):
    B, S, W3 = qkv.shape
    W = W3 // 3
    npair = W // LANES
    tq = min(tq, S)
    assert S % tq == 0 and W % LANES == 0, (S, tq, W)
    nq = S // tq
    below = [(i, t, 2 * tq) for i in range(nq) for t in range(i)]
    n_plain = len(below) - len(below) % 2
    masked = below[n_plain:] + [(i, i, 0) for i in range(nq)]
    masked += [(nq - 1, nq - 1, -2 * tq)] * (len(masked) % 2)
    items = below[:n_plain] + masked
    tabs = [jnp.asarray([it[n] for it in items], jnp.int32) for n in range(3)]
    seq = lambda col: pl.BlockSpec((1, S, LANES), lambda b, p, *_: (b, 0, col(p)))
    return pl.pallas_call(
        functools.partial(_fox_kernel, tq=tq, n_plain=n_plain, n_masked=len(masked),
                          prep_rows=tq * (2 if nq % 2 == 0 else 1), scale=HEAD_DIM ** -0.5),
        grid_spec=pltpu.PrefetchScalarGridSpec(
            num_scalar_prefetch=3,
            grid=(B, npair),
            in_specs=[seq(lambda p: p), seq(lambda p: npair + p), seq(lambda p: 2 * npair + p), seq(lambda p: 0)],
            out_specs=seq(lambda p: p),
            scratch_shapes=[pltpu.VMEM((2, S, LANES), BF16), pltpu.VMEM((2, S, LANES), BF16),
                            pltpu.VMEM((2, LANES, S), BF16),
                            pltpu.VMEM((2, 2, tq, tq), F32), pltpu.VMEM((2, 2, tq, tq), BF16),
                            pltpu.VMEM((nq, 2, LANES, tq), F32), pltpu.VMEM((nq, 2, 1, tq), F32)]),
        out_shape=jax.ShapeDtypeStruct((B, S, W), F32),
        compiler_params=_params(("parallel", "parallel")),
        name="fox_attn",
    )(*tabs, qkv, qkv, qkv, c)


def _rwkv_prep_kernel(u_ref, up_ref, mu_ref, w0_ref, a0_ref, kk_ref, ka_ref, rk_ref, wup_ref, aup_ref, gup_ref,
                      seg_ref, segt_ref,
                      r_o, lw_o, k_o, v_o, kkn_o, b_o, g_o, bonus_o, *, tm, width):
    i = pl.program_id(1)
    first_row = lax.broadcasted_iota(jnp.int32, (tm, 1), 0) == 0
    has_prev = i > 0

    def shifted(lo, hi):
        u = u_ref[0, :, lo:hi]
        prev_last = jnp.where(has_prev, up_ref[0, 7:8, lo:hi], 0.0)
        u_prev = jnp.where(first_row, prev_last, pltpu.roll(u, 1, 0))
        return u + mu_ref[:, lo:hi] * (u_prev - u)

    def head_sum(t):
        per_head = _split_dot(t, seg_ref[...], 2)
        return _split_dot(per_head, segt_ref[...], 2)

    W = width
    r = shifted(0, W)
    k = shifted(W, 2 * W)
    v = shifted(2 * W, 3 * W)
    wd = shifted(3 * W, 3 * W + LANES)
    ad = shifted(3 * W + LANES, 3 * W + 2 * LANES)
    gd = shifted(3 * W + 2 * LANES, 3 * W + 4 * LANES)

    wlin = w0_ref[...] + jnp.dot(jnp.tanh(wd).astype(BF16), wup_ref[...], preferred_element_type=F32)
    z = -wlin
    w = -(jnp.maximum(z, 0.0) + jnp.log(1.0 + jnp.exp(-jnp.abs(z)))) - 0.5
    lw_o[0] = -jnp.exp(w)
    a = jax.nn.sigmoid(a0_ref[...] + jnp.dot(ad.astype(BF16), aup_ref[...], preferred_element_type=F32))
    g = jnp.dot(jax.nn.sigmoid(gd).astype(BF16), gup_ref[...], preferred_element_type=F32)
    g_o[0] = g.astype(g_o.dtype)
    kk = k * kk_ref[...]
    kkn = kk / jnp.maximum(jnp.sqrt(head_sum(kk * kk)), 1e-12)
    k2 = k * (1.0 + (a - 1.0) * ka_ref[...])
    r_o[0] = r.astype(r_o.dtype)
    k_o[0] = k2.astype(k_o.dtype)
    v_o[0] = v.astype(v_o.dtype)
    kkn_o[0] = kkn.astype(kkn_o.dtype)
    b_o[0] = (a * kkn).astype(b_o.dtype)
    bonus_o[0] = (head_sum(r * k2 * rk_ref[...]) * v).astype(bonus_o.dtype)


def _rwkv_prep(u, mu, w0, a0, k_k, k_a, r_k, w_up, a_up, g_up, tm=256):
    B, S, UW = u.shape
    W = w0.shape[1]
    tm = min(tm, S)
    assert S % tm == 0 and tm % 8 == 0 and W // HEAD_DIM <= LANES, (S, tm, W)
    head_of = jnp.arange(W) // HEAD_DIM
    seg = (head_of[:, None] == jnp.arange(LANES)[None, :]).astype(BF16)
    segt = seg.T
    blk = lambda b, i: (b, i, 0)
    vec = lambda b, i: (0, 0)
    out = [jax.ShapeDtypeStruct((B, S, W), F32 if n == 1 else BF16) for n in range(8)]
    return pl.pallas_call(
        functools.partial(_rwkv_prep_kernel, tm=tm, width=W),
        grid=(B, S // tm),
        in_specs=[pl.BlockSpec((1, tm, UW), blk),
                  pl.BlockSpec((1, 8, UW), lambda b, i: (b, jnp.maximum(i * (tm // 8) - 1, 0), 0)),
                  pl.BlockSpec((1, UW), vec),
                  pl.BlockSpec((1, W), vec), pl.BlockSpec((1, W), vec), pl.BlockSpec((1, W), vec),
                  pl.BlockSpec((1, W), vec), pl.BlockSpec((1, W), vec),
                  pl.BlockSpec((LANES, W), vec), pl.BlockSpec((LANES, W), vec), pl.BlockSpec((2 * LANES, W), vec),
                  pl.BlockSpec((W, LANES), vec), pl.BlockSpec((LANES, W), vec)],
        out_specs=[pl.BlockSpec((1, tm, W), blk)] * 8,
        out_shape=out,
        compiler_params=_params(("parallel", "arbitrary")),
        name="rwkv_prep",
    )(u, u, mu, w0, a0, k_k, k_a, r_k, w_up, a_up, g_up, seg, segt)


def _rwkv_scan_kernel(r_ref, lw_ref, k_ref, v_ref, kkn_ref, b_ref, g_ref, bonus_ref, gng_ref, gnb_ref,
                      tri_ref, o_ref, s_ref, *, nchunk, npair, nbatch):
    C = CHUNK

    @pl.when(pl.program_id(2) == 0)
    def _():
        s_ref[...] = jnp.zeros_like(s_ref)

    t_i = lax.broadcasted_iota(jnp.int32, (C, LANES), 0)
    lane = lax.broadcasted_iota(jnp.int32, (C, LANES), 1)
    s_i = lane & (HEAD_DIM - 1)
    low = lane < HEAD_DIM
    strict = s_i < t_i
    incl = s_i <= t_i
    eye = (s_i == t_i).astype(F32)
    levels = []
    k_ = 0
    while (1 << k_) < C:
        levels.append(((t_i >> k_) == ((s_i >> k_) + 1)) & ((t_i >> (k_ + 1)) == (s_i >> (k_ + 1))))
        k_ += 1
    blk_r = lax.broadcasted_iota(jnp.int32, (LANES, LANES), 0) // HEAD_DIM
    blk_c = lax.broadcasted_iota(jnp.int32, (LANES, LANES), 1) // HEAD_DIM
    same_head = blk_r == blk_c

    def block_diag(y):
        zero = jnp.zeros_like(y)
        return jnp.concatenate([jnp.where(low, y, zero), jnp.where(low, zero, y)], axis=0)

    def head_dot(x, y):
        return jnp.dot(x.astype(BF16), block_diag(y.astype(BF16)), preferred_element_type=F32)

    def head_mean(t):
        lo = jnp.sum(jnp.where(low, t, 0.0), axis=1, keepdims=True)
        hi = jnp.sum(jnp.where(low, 0.0, t), axis=1, keepdims=True)
        return jnp.where(low, lo, hi) * (1.0 / HEAD_DIM)

    tri = tri_ref[...]

    def chunk(c, carry):
        sl = pl.ds(pl.multiple_of(c * C, C), C)
        pairs = range(nbatch * npair)
        col = lambda t, p: t[:, p * LANES:(p + 1) * LANES]
        rows = lambda ref: jnp.concatenate([ref[bb, sl, :] for bb in range(nbatch)], axis=1)
        r = rows(r_ref)
        lw = rows(lw_ref)
        k = rows(k_ref)
        v = rows(v_ref)
        kkn = rows(kkn_ref)
        b = rows(b_ref)

        cum = _split_dot_rhs(tri, lw, 2)
        cum_last = cum[C - 1:C, :]
        g_inv = jnp.exp(-cum)
        g_tail = jnp.exp(cum_last - cum)
        a_t = (-kkn * jnp.exp(cum - lw)).astype(BF16)
        r_t = (r * jnp.exp(cum)).astype(BF16)
        k_t = (k * g_inv).astype(BF16)
        b_t = (b * g_inv).astype(BF16)
        k_tail = (k * g_tail).astype(BF16)
        b_tail = (b * g_tail).astype(BF16)
        v16 = v.astype(BF16)
        decay_last = jnp.exp(cum_last)

        s0 = [s_ref[p] for p in pairs]
        lhs = [jnp.concatenate([col(a_t, p), col(r_t, p)], axis=0) for p in pairs]
        amat = [_dot_nt(lhs[p], jnp.concatenate([block_diag(col(k_t, p)), block_diag(col(b_t, p))], axis=0))
                for p in pairs]
        s_proj = [_dot_nt(lhs[p], s0[p].astype(BF16)) for p in pairs]
        n_ab = [jnp.where(strict, amat[p][:C, LANES:], 0.0) for p in pairs]
        inv = [eye + jnp.where(levels[0], n_ab[p], 0.0) for p in pairs]
        for lvl in levels[1:]:
            x = [head_dot(inv[p], jnp.where(lvl, n_ab[p], 0.0)) for p in pairs]
            inv = [inv[p] + head_dot(x[p], inv[p]) for p in pairs]

        vbd = [block_diag(col(v16, p)) for p in pairs]
        rhs_u = [s_proj[p][:C] + jnp.dot(jnp.where(strict, amat[p][:C, :LANES], 0.0).astype(BF16), vbd[p],
                                         preferred_element_type=F32) for p in pairs]
        u = [head_dot(inv[p], rhs_u[p]) for p in pairs]
        u16 = [u[p].astype(BF16) for p in pairs]
        y = [s_proj[p][C:] + jnp.dot(jnp.where(jnp.concatenate([incl, incl], axis=1), amat[p][C:], 0.0).astype(BF16),
                                     jnp.concatenate([vbd[p], block_diag(u16[p])], axis=0),
                                     preferred_element_type=F32) for p in pairs]
        s_upd = [_dot_tn(jnp.concatenate([col(v16, p), u16[p]], axis=0),
                         jnp.concatenate([col(k_tail, p), col(b_tail, p)], axis=0)) for p in pairs]
        s_ref[...] = jnp.stack([s0[p] * col(decay_last, p) + jnp.where(same_head, s_upd[p], 0.0) for p in pairs],
                               axis=0)

        yc = [y[p] - head_mean(y[p]) for p in pairs]
        var = [head_mean(yc[p] * yc[p]) for p in pairs]
        yn = jnp.concatenate([yc[p] * lax.rsqrt(var[p] + GN_EPS) for p in pairs], axis=1)
        gn_g = jnp.concatenate([gng_ref[...]] * nbatch, axis=1)
        gn_b = jnp.concatenate([gnb_ref[...]] * nbatch, axis=1)
        out = ((yn * gn_g + gn_b + rows(bonus_ref)) * rows(g_ref)).astype(o_ref.dtype)
        width = npair * LANES
        for bb in range(nbatch):
            o_ref[bb, sl, :] = out[:, bb * width:(bb + 1) * width]
        return carry

    lax.fori_loop(0, nchunk, chunk, 0)


def _rwkv_scan(r, lw, k, v, kkn, b, g, bonus, gn_g, gn_b, tc=256, npair=8, nbatch=4):
    B, S, W = r.shape
    tc = min(tc, S)
    npair = min(npair, W // LANES)
    nbatch = nbatch if B % nbatch == 0 else 1
    assert S % tc == 0 and tc % CHUNK == 0 and W % (npair * LANES) == 0, (S, tc, W, npair)
    C = CHUNK
    tri = jnp.tril(jnp.ones((C, C), F32)).astype(BF16)
    wl = npair * LANES
    seq = pl.BlockSpec((nbatch, tc, wl), lambda bb, p, t: (bb, t, p))
    vec = pl.BlockSpec((1, wl), lambda bb, p, t: (0, p))
    return pl.pallas_call(
        functools.partial(_rwkv_scan_kernel, nchunk=tc // C, npair=npair, nbatch=nbatch),
        grid=(B // nbatch, W // wl, S // tc),
        in_specs=[seq] * 8 + [vec, vec, pl.BlockSpec((C, C), lambda bb, p, t: (0, 0))],
        out_specs=seq,
        out_shape=jax.ShapeDtypeStruct((B, S, W), BF16),
        scratch_shapes=[pltpu.VMEM((nbatch * npair, LANES, LANES), F32)],
        compiler_params=_params(("parallel", "parallel", "arbitrary")),
        name="rwkv_scan",
    )(r, lw, k, v, kkn, b, g, bonus, gn_g, gn_b, tri)


def _mix_kernel(fo_ref, ro_ref, x_ref, wf_ref, wr_ref, fg_ref, g_ref, b_ref, o_ref, *, alpha):
    fo = fo_ref[...]
    fo = fo * lax.rsqrt(jnp.mean(fo * fo, axis=-1, keepdims=True) + FOX_EPS) * fg_ref[...]
    mix = jnp.dot(fo.astype(BF16), wf_ref[...], preferred_element_type=F32)
    mix = mix + jnp.dot(ro_ref[...].astype(BF16), wr_ref[...], preferred_element_type=F32)
    o_ref[...] = _layer_norm(alpha * x_ref[...] + mix, g_ref[...], b_ref[...])


def _mix(fo, ro, x, w_mix, layer, fox_g, g, b, alpha, tm=512):
    T, D = x.shape
    WF, WR = fo.shape[1], ro.shape[1]
    tm = min(tm, T)
    assert WF == WR and T % tm == 0, (WF, WR, T, tm)
    row = lambda i: (i, 0)
    vec = lambda i: (0, 0)
    return pl.pallas_call(
        functools.partial(_mix_kernel, alpha=alpha),
        grid=(T // tm,),
        in_specs=[pl.BlockSpec((tm, WF), row), pl.BlockSpec((tm, WR), row), pl.BlockSpec((tm, D), row),
                  pl.BlockSpec((None, WF, D), lambda i: (layer, 0, 0)),
                  pl.BlockSpec((None, WR, D), lambda i: (layer, 1, 0)),
                  pl.BlockSpec((1, WF), vec), pl.BlockSpec((1, D), vec), pl.BlockSpec((1, D), vec)],
        out_specs=pl.BlockSpec((tm, D), row),
        out_shape=jax.ShapeDtypeStruct((T, D), F32),
        compiler_params=_params(("parallel",)),
        name="mix_out",
    )(fo, ro, x, w_mix, w_mix, fox_g, g, b)


def _xattn_kernel(x_ref, wq_ref, kv_k_ref, kv_v_ref, wo_ref, g_ref, b_ref, o_ref, *, alpha, heads):
    x = x_ref[0]
    D = x.shape[1]
    dh = D // heads
    cols = [slice(h * dh, (h + 1) * dh) for h in range(heads)]
    q = jnp.dot(x.astype(BF16), wq_ref[...], preferred_element_type=F32).astype(BF16)
    s = [_dot_nt(q[:, c], kv_k_ref[0, :, c]) * (dh ** -0.5) for c in cols]
    e = [jnp.exp(sh - jnp.max(sh, axis=1, keepdims=True)) for sh in s]
    p = [(eh / jnp.sum(eh, axis=1, keepdims=True)).astype(BF16) for eh in e]
    o = [jnp.dot(ph, kv_v_ref[0, :, c], preferred_element_type=F32).astype(BF16) for ph, c in zip(p, cols)]
    xa = jnp.dot(jnp.concatenate(o, axis=1), wo_ref[...], preferred_element_type=F32)
    o_ref[0] = _layer_norm(alpha * x + xa, g_ref[...], b_ref[...])


def _xattn(x, kv, wq, wo, layer, g, b, alpha, tm=512):
    B, S, D = x.shape
    M = kv.shape[1]
    tm = min(tm, S)
    vec = lambda bb, i: (0, 0)
    return pl.pallas_call(
        functools.partial(_xattn_kernel, alpha=alpha, heads=XA_HEADS),
        grid=(B, S // tm),
        in_specs=[pl.BlockSpec((1, tm, D), lambda bb, i: (bb, i, 0)),
                  pl.BlockSpec((None, D, D), lambda bb, i: (layer, 0, 0), pipeline_mode=pl.Buffered(1)),
                  pl.BlockSpec((1, M, D), lambda bb, i: (bb, 0, 0)),
                  pl.BlockSpec((1, M, D), lambda bb, i: (bb, 0, 1)),
                  pl.BlockSpec((None, D, D), lambda bb, i: (layer, 0, 0), pipeline_mode=pl.Buffered(1)),
                  pl.BlockSpec((1, D), vec), pl.BlockSpec((1, D), vec)],
        out_specs=pl.BlockSpec((1, tm, D), lambda bb, i: (bb, i, 0)),
        out_shape=jax.ShapeDtypeStruct((B, S, D), F32),
        compiler_params=_params(("parallel", "arbitrary")),
        name="xattn",
    )(x, wq, kv, kv, wo, g, b)


def _pad_cols(w, n):
    return jnp.pad(w, ((0, 0), (0, n - w.shape[1])))


def _pad_rows(w, n):
    return jnp.pad(w, ((0, n - w.shape[0]), (0, 0)))


def kernel(x, mem, ln_g, ln_b, ffn1_w13, ffn1_w2, w_in, fox_f_bias, fox_out_g, rwkv_mu, rwkv_w0, rwkv_w_up, rwkv_a0, rwkv_a_up, rwkv_g_up, rwkv_k_k, rwkv_k_a, rwkv_r_k, rwkv_gn_g, rwkv_gn_b, w_mix_out, xa_wq, xa_wkv, xa_wo, ffn2_w13, ffn2_w2):
    B, S, D = x.shape
    depth = ln_g.shape[0]
    M = mem.shape[1]
    alpha = (2.0 * depth) ** 0.25
    fox_heads = fox_f_bias.shape[1]
    WF = fox_heads * HEAD_DIM
    WR = rwkv_w0.shape[1]
    decay_lora, aaa_lora, gate_lora = rwkv_w_up.shape[1], rwkv_a_up.shape[1], rwkv_g_up.shape[1]
    c1 = 3 * WF
    c2 = c1 + fox_heads
    T = B * S
    row = lambda t: t.reshape(1, -1)

    h = x.reshape(T, D)
    mem_b = mem.reshape(B * M, D).astype(BF16)
    ffn1_w13, ffn1_w2, ffn2_w13, ffn2_w2, w_mix_out, xa_wq, xa_wkv, xa_wo = (
        w.astype(BF16) for w in (ffn1_w13, ffn1_w2, ffn2_w13, ffn2_w2, w_mix_out, xa_wq, xa_wkv, xa_wo))
    for l in range(depth):
        h, hb = _ffn(h, ffn1_w13, ffn1_w2, l, row(ln_g[l, 0]), row(ln_b[l, 0]), alpha)

        wl = w_in[l].astype(BF16)
        w_fox = wl[:, :c1]
        w_gate = _pad_cols(wl[:, c1:c2], LANES)
        o = c2 + 3 * WR
        w_rw = jnp.concatenate([wl[:, c2:o],
                                _pad_cols(wl[:, o:o + decay_lora], LANES),
                                _pad_cols(wl[:, o + decay_lora:o + decay_lora + aaa_lora], LANES),
                                _pad_cols(wl[:, o + decay_lora + aaa_lora:], 2 * LANES)], axis=1)
        mu = rwkv_mu[l]
        o = 3 * WR
        mu_p = jnp.concatenate([mu[:o],
                                jnp.pad(mu[o:o + decay_lora], (0, LANES - decay_lora)),
                                jnp.pad(mu[o + decay_lora:o + decay_lora + aaa_lora], (0, LANES - aaa_lora)),
                                jnp.pad(mu[o + decay_lora + aaa_lora:], (0, 2 * LANES - gate_lora))])

        qkv = _matmul(hb, w_fox, BF16).reshape(B, S, c1)
        u = _matmul(hb, w_rw, F32).reshape(B, S, -1)
        bias = jnp.pad(fox_f_bias[l], (0, LANES - fox_heads)).reshape(1, LANES)
        fo = _fox_attention(qkv, _gate_cumsum(hb.reshape(B, S, D), w_gate, bias))

        r, lw, k2, v, kkn, bb, g, bonus = _rwkv_prep(
            u, row(mu_p), row(rwkv_w0[l]), row(rwkv_a0[l]), row(rwkv_k_k[l]), row(rwkv_k_a[l]),
            row(rwkv_r_k[l]),
            _pad_rows(rwkv_w_up[l], LANES).astype(BF16), _pad_rows(rwkv_a_up[l], LANES).astype(BF16),
            _pad_rows(rwkv_g_up[l], 2 * LANES).astype(BF16))
        ro = _rwkv_scan(r, lw, k2, v, kkn, bb, g, bonus, row(rwkv_gn_g[l]), row(rwkv_gn_b[l]))

        h = _mix(fo.reshape(T, WF), ro.reshape(T, WR), h, w_mix_out, l, row(fox_out_g[l]),
                 row(ln_g[l, 1]), row(ln_b[l, 1]), alpha)

        kv = _matmul(mem_b, xa_wkv, BF16, layer=l).reshape(B, M, 2 * D)
        h = _xattn(h.reshape(B, S, D), kv, xa_wq, xa_wo, l,
                   row(ln_g[l, 2]), row(ln_b[l, 2]), alpha).reshape(T, D)

        h, _ = _ffn(h, ffn2_w13, ffn2_w2, l, row(ln_g[l, 3]), row(ln_b[l, 3]), alpha)
    return h.reshape(B, S, D)
```
